```python
import jax, jax.numpy as jnp
from jax import lax
import numpy as np

D_MODEL = 2048
BATCH = 1
SEQ = 16384
DEPTH = 2

N_MEM = 256
D_MIX = D_MODEL
GROUP_W = D_MIX // 4
LRU_W = GROUP_W
LRU_BLOCKS = 8
LRU_BLOCK_W = LRU_W // LRU_BLOCKS
CONV_W = 4
LRU_C = 8.0
RWKV_W = GROUP_W
RWKV_HEAD = 64
RWKV_HEADS = RWKV_W // RWKV_HEAD
DECAY_LORA = 32
ICLR_LORA = 32
RWKV_SHIFT_W = 3 * RWKV_W + DECAY_LORA + ICLR_LORA
RWKV_GN_EPS = 64e-5
SWA_HEAD = 64
SWA_HEADS = GROUP_W // SWA_HEAD
SWA_KV_HEADS = 2
SWA_Q_PER_KV = SWA_HEADS // SWA_KV_HEADS
WINDOW = 128
BLOCK = 128
MEM_HEADS = 4
MEM_HEAD = GROUP_W // MEM_HEADS
EPS = 1e-6

IN_SPLITS = (
    LRU_W, LRU_W,
    RWKV_SHIFT_W, RWKV_W,
    GROUP_W, SWA_KV_HEADS * SWA_HEAD, SWA_KV_HEADS * SWA_HEAD, GROUP_W,
    GROUP_W, GROUP_W,
)
IN_WIDTH = sum(IN_SPLITS)

kernel_name = "hybrid_parallel_heads_lru_rwkv7_swa_mem"


def _split(t, sizes):
    offs = [int(o) for o in np.cumsum(sizes)[:-1]]
    return jnp.split(t, offs, axis=-1)


def rmsnorm(x, g, eps=EPS):
    xf = x.astype(jnp.float32)
    y = xf * lax.rsqrt(jnp.mean(xf * xf, axis=-1, keepdims=True) + eps)
    return (y * g.astype(jnp.float32)).astype(x.dtype)


def token_shift(t):
    return jnp.pad(t, ((0, 0), (1, 0), (0, 0)))[:, :-1]


def causal_dwconv(x, w, b):
    y = lax.conv_general_dilated(
        x, w[:, None, :].astype(x.dtype), window_strides=(1,), padding=[(CONV_W - 1, 0)],
        dimension_numbers=("NWC", "WIO", "NWC"), feature_group_count=x.shape[-1])
    return y + b.astype(x.dtype)


def rg_lru(xc, w_a, b_a, w_x, b_x, lam):
    B, S, C = xc.shape
    xf = xc.astype(jnp.float32)
    xb = xf.reshape(B, S, LRU_BLOCKS, LRU_BLOCK_W)
    r = jax.nn.sigmoid(jnp.einsum("bshi,hij->bshj", xb, w_a.astype(jnp.float32)).reshape(B, S, C) + b_a)
    i = jax.nn.sigmoid(jnp.einsum("bshi,hij->bshj", xb, w_x.astype(jnp.float32)).reshape(B, S, C) + b_x)
    log_a = -LRU_C * r * jax.nn.softplus(-lam.astype(jnp.float32))
    a = jnp.exp(log_a)
    mult = jnp.sqrt(jnp.maximum(-jnp.expm1(2.0 * log_a), 1e-12))
    u = mult * (i * xf)

    def combine(c1, c2):
        a1, b1 = c1
        a2, b2 = c2
        return a1 * a2, a2 * b1 + b2

    _, h = lax.associative_scan(combine, (a, u), axis=1)
    return h


def rwkv7_time_mix(p_shift, mu, w0, w_up, a0, a_up, k_k, k_a, r_k, gn_g, gn_b):
    B, S, _ = p_shift.shape
    H, N = RWKV_HEADS, RWKV_HEAD
    pf = p_shift.astype(jnp.float32)
    pm = pf + (token_shift(pf) - pf) * mu
    r, k, v, wd, ad = _split(pm, (RWKV_W, RWKV_W, RWKV_W, DECAY_LORA, ICLR_LORA))
    w = -jax.nn.softplus(-(w0 + jnp.tanh(wd) @ w_up)) - 0.5
    decay = jnp.exp(-jnp.exp(w))
    a = jax.nn.sigmoid(a0 + ad @ a_up)
    kk = (k * k_k).reshape(B, S, H, N)
    kk = kk / jnp.maximum(jnp.linalg.norm(kk, axis=-1, keepdims=True), 1e-12)
    k = k * (1.0 + (a - 1.0) * k_a)
    hd = lambda t: t.reshape(B, S, H, N)
    r_h, w_h, k_h, v_h, a_h = hd(r), hd(decay), hd(k), hd(v), hd(a)
    b_h = kk * a_h

    def step(state, inp):
        r_t, w_t, k_t, v_t, kk_t, b_t = inp
        sa = jnp.einsum("bhij,bhj->bhi", state, -kk_t)
        state = (state * w_t[:, :, None, :] + sa[..., None] * b_t[:, :, None, :]
                 + v_t[..., None] * k_t[:, :, None, :])
        return state, jnp.einsum("bhij,bhj->bhi", state, r_t)

    xs = tuple(jnp.moveaxis(t, 1, 0) for t in (r_h, w_h, k_h, v_h, kk, b_h))
    _, y = lax.scan(step, jnp.zeros((B, H, N, N), jnp.float32), xs)
    y = jnp.moveaxis(y, 0, 1)
    mean = jnp.mean(y, axis=-1, keepdims=True)
    var = jnp.mean(jnp.square(y - mean), axis=-1, keepdims=True)
    y = ((y - mean) * lax.rsqrt(var + RWKV_GN_EPS)).reshape(B, S, RWKV_W) * gn_g + gn_b
    bonus = jnp.sum(r_h * k_h * r_k, axis=-1, keepdims=True) * v_h
    return y + bonus.reshape(B, S, RWKV_W)


def sliding_window_attention(q, k, v, q_g, k_g, sinks):
    B, S, _ = q.shape
    nb = S // BLOCK
    f32 = jnp.float32
    q = rmsnorm(q.reshape(B, S, SWA_HEADS, SWA_HEAD).astype(f32), q_g)
    k = rmsnorm(k.reshape(B, S, SWA_KV_HEADS, SWA_HEAD).astype(f32), k_g)
    v = v.reshape(B, S, SWA_KV_HEADS, SWA_HEAD).astype(f32)
    qb = q.reshape(B, nb, BLOCK, SWA_KV_HEADS, SWA_Q_PER_KV, SWA_HEAD)
    kb = k.reshape(B, nb, BLOCK, SWA_KV_HEADS, SWA_HEAD)
    vb = v.reshape(B, nb, BLOCK, SWA_KV_HEADS, SWA_HEAD)

    def with_prev(t):
        prev = jnp.pad(t, ((0, 0), (1, 0), (0, 0), (0, 0), (0, 0)))[:, :-1]
        return jnp.concatenate([prev, t], axis=2)

    kw, vw = with_prev(kb), with_prev(vb)
    scores = jnp.einsum("bnqkgd,bnskd->bnkgqs", qb, kw) * (SWA_HEAD ** -0.5)
    qi = jnp.arange(BLOCK)[:, None]
    sj = jnp.arange(2 * BLOCK)[None, :] - BLOCK
    rel = qi - sj
    band = (rel >= 0) & (rel < WINDOW)
    key_ok = (jnp.arange(nb)[:, None] * BLOCK + sj) >= 0
    mask = band[None] & key_ok[:, None, :]
    scores = jnp.where(mask[None, :, None, None], scores, -jnp.inf)
    sink = jnp.broadcast_to(
        sinks.astype(f32).reshape(SWA_KV_HEADS, SWA_Q_PER_KV)[None, None, :, :, None, None],
        scores.shape[:-1] + (1,))
    probs = jax.nn.softmax(jnp.concatenate([scores, sink], axis=-1), axis=-1)[..., :-1]
    out = jnp.einsum("bnkgqs,bnskd->bnqkgd", probs, vw)
    return out.reshape(B, S, GROUP_W)


def memory_cross_attention(q, mem, mem_g, w_kv, q_g, k_g):
    B, S, _ = q.shape
    M = mem.shape[1]
    f32 = jnp.float32
    kv = rmsnorm(mem, mem_g) @ w_kv
    mk, mv = jnp.split(kv, 2, axis=-1)
    q = rmsnorm(q.reshape(B, S, MEM_HEADS, MEM_HEAD).astype(f32), q_g)
    mk = rmsnorm(mk.reshape(B, M, MEM_HEADS, MEM_HEAD).astype(f32), k_g)
    mv = mv.reshape(B, M, MEM_HEADS, MEM_HEAD).astype(f32)
    s = jnp.einsum("bshd,bmhd->bhsm", q, mk) * (MEM_HEAD ** -0.5)
    p = jax.nn.softmax(s, axis=-1)
    return jnp.einsum("bhsm,bmhd->bshd", p, mv).reshape(B, S, GROUP_W)


def setup_inputs(seed: int = 0) -> dict:
    key = jax.random.key(seed)
    ks = jax.random.split(key, 32)
    f32 = jnp.float32
    L = DEPTH

    def nrm(k, shape, s):
        return s * jax.random.normal(k, shape, f32)

    a_c = jax.random.uniform(ks[10], (L, LRU_W), f32, 0.9, 0.999)
    p = a_c ** (1.0 / LRU_C)
    return {
        "x": nrm(ks[0], (BATCH, SEQ, D_MODEL), 1.0),
        "mem": nrm(ks[1], (BATCH, N_MEM, D_MODEL), 1.0),
        "norm_g": 1.0 + nrm(ks[2], (L, D_MODEL), 0.02),
        "w_in": nrm(ks[3], (L, D_MODEL, IN_WIDTH), D_MODEL ** -0.5),
        "conv_w": nrm(ks[4], (L, CONV_W, LRU_W), CONV_W ** -0.5),
        "conv_b": nrm(ks[5], (L, LRU_W), 0.01),
        "lru_wa": nrm(ks[6], (L, LRU_BLOCKS, LRU_BLOCK_W, LRU_BLOCK_W), LRU_BLOCK_W ** -0.5),
        "lru_ba": nrm(ks[7], (L, LRU_W), 0.01),
        "lru_wx": nrm(ks[8], (L, LRU_BLOCKS, LRU_BLOCK_W, LRU_BLOCK_W), LRU_BLOCK_W ** -0.5),
        "lru_bx": nrm(ks[9], (L, LRU_W), 0.01),
        "lru_lambda": jnp.log(p) - jnp.log1p(-p),
        "rw_mu": jax.random.uniform(ks[11], (L, RWKV_SHIFT_W), f32, 0.2, 0.8),
        "rw_w0": jax.random.uniform(ks[12], (L, RWKV_W), f32, -6.0, -1.0),
        "rw_w_up": nrm(ks[13], (L, DECAY_LORA, RWKV_W), 0.5 * DECAY_LORA ** -0.5),
        "rw_a0": nrm(ks[14], (L, RWKV_W), 0.5),
        "rw_a_up": nrm(ks[15], (L, ICLR_LORA, RWKV_W), 0.5 * ICLR_LORA ** -0.5),
        "rw_k_k": 0.85 + nrm(ks[16], (L, RWKV_W), 0.05),
        "rw_k_a": 1.0 + nrm(ks[17], (L, RWKV_W), 0.05),
        "rw_r_k": nrm(ks[18], (L, RWKV_HEADS, RWKV_HEAD), 0.1),
        "rw_gn_g": 1.0 + nrm(ks[19], (L, RWKV_W), 0.02),
        "rw_gn_b": nrm(ks[20], (L, RWKV_W), 0.01),
        "swa_q_g": 1.0 + nrm(ks[21], (L, SWA_HEAD), 0.02),
        "swa_k_g": 1.0 + nrm(ks[22], (L, SWA_HEAD), 0.02),
        "swa_sinks": nrm(ks[23], (L, SWA_HEADS), 0.5),
        "mem_norm_g": 1.0 + nrm(ks[24], (L, D_MODEL), 0.02),
        "w_mem_kv": nrm(ks[25], (L, D_MODEL, 2 * GROUP_W), D_MODEL ** -0.5),
        "mem_q_g": 1.0 + nrm(ks[26], (L, MEM_HEAD), 0.02),
        "mem_k_g": 1.0 + nrm(ks[27], (L, MEM_HEAD), 0.02),
        "w_out": nrm(ks[28], (L, D_MIX, D_MODEL), 0.5 * D_MIX ** -0.5),
    }


def reference(x, mem, norm_g, w_in, conv_w, conv_b, lru_wa, lru_ba, lru_wx, lru_bx, lru_lambda,
              rw_mu, rw_w0, rw_w_up, rw_a0, rw_a_up, rw_k_k, rw_k_a, rw_r_k, rw_gn_g, rw_gn_b,
              swa_q_g, swa_k_g, swa_sinks, mem_norm_g, w_mem_kv, mem_q_g, mem_k_g, w_out):
    f32 = jnp.float32
    for l in range(DEPTH):
        h = rmsnorm(x, norm_g[l])
        p = h @ w_in[l]
        (lru_x, lru_g, rw_p, rw_g, sq, sk, sv, sg, mq, mg) = _split(p, IN_SPLITS)
        y_lru = rg_lru(causal_dwconv(lru_x, conv_w[l], conv_b[l]),
                       lru_wa[l], lru_ba[l], lru_wx[l], lru_bx[l], lru_lambda[l])
        y_rw = rwkv7_time_mix(rw_p, rw_mu[l], rw_w0[l], rw_w_up[l], rw_a0[l], rw_a_up[l],
                              rw_k_k[l], rw_k_a[l], rw_r_k[l], rw_gn_g[l], rw_gn_b[l])
        y_swa = sliding_window_attention(sq, sk, sv, swa_q_g[l], swa_k_g[l], swa_sinks[l])
        y_mem = memory_cross_attention(mq, mem, mem_norm_g[l], w_mem_kv[l], mem_q_g[l], mem_k_g[l])
        o = jnp.concatenate([
            y_lru * jax.nn.silu(lru_g.astype(f32)),
            y_rw * jax.nn.silu(rw_g.astype(f32)),
            y_swa * jax.nn.silu(sg.astype(f32)),
            y_mem * jax.nn.silu(mg.astype(f32)),
        ], axis=-1).astype(x.dtype)
        x = x + o @ w_out[l]
    return x
```

```python
import functools

import jax
import jax.numpy as jnp
from jax import lax
from jax.experimental import pallas as pl
from jax.experimental.pallas import tpu as pltpu

F32 = jnp.float32
BF16 = jnp.bfloat16

LANES = 128
GROUP_W = 512
HEAD64 = 64
CHUNK = 64
CONV_W = 4
LRU_C = 8.0
LORA = 32
EPS = 1e-6
RWKV_GN_EPS = 64e-5
WINDOW = 128
N_MEM = 256
MEM_HEADS = 4
NEG_BIG = -1e30
VMEM_LIMIT = 56 * 1024 * 1024

COL_LRU_X, COL_LRU_G = 0, 512
COL_RW_R, COL_RW_K, COL_RW_V, COL_RW_G = 1024, 1536, 2048, 2560
COL_SWA_Q, COL_SWA_G = 3072, 3584
COL_MEM_Q, COL_MEM_G = 4096, 4608
COL_SWA_KV, COL_RW_LORA = 5120, 5376
IN_PAD = 5632

NN = (((1,), (0,)), ((), ()))
NT = (((1,), (1,)), ((), ()))
TN = (((0,), (0,)), ((), ()))


def _dot(a, b, dims=NN):
    return lax.dot_general(a, b, dims, preferred_element_type=F32)


def _split(a):
    hi = a.astype(BF16)
    lo = (a - hi.astype(F32)).astype(BF16)
    return hi, lo


def _mm(a, b, dims=NN, passes=1):
    if passes == 1:
        return _dot(a.astype(BF16), b.astype(BF16), dims)
    ah, al = _split(a)
    bh, bl = _split(b)
    return _dot(ah, bh, dims) + (_dot(ah, bl, dims) + _dot(al, bh, dims))


def _mm_exact_rhs(a, b_bf16):
    ah, al = _split(a)
    return _dot(ah, b_bf16) + _dot(al, b_bf16)


def _iota2(shape, axis):
    return lax.broadcasted_iota(jnp.int32, shape, axis)


def _head_ones():
    r = _iota2((LANES, LANES), 0) // HEAD64
    c = _iota2((LANES, LANES), 1) // HEAD64
    return (r == c).astype(BF16)


def _segsum64(x, ones_bd):
    n, w = x.shape
    cols = [_mm_exact_rhs(x[:, j:j + LANES], ones_bd) for j in range(0, w, LANES)]
    return cols[0] if len(cols) == 1 else jnp.concatenate(cols, axis=1)


def _softplus(z):
    return jnp.maximum(z, 0.0) + jnp.log1p(jnp.exp(-jnp.abs(z)))


def _silu(z):
    return z * jax.nn.sigmoid(z)


def _inproj_kernel(x_ref, g_ref, w_ref, o_ref, h_ref):
    @pl.when(pl.program_id(1) == 0)
    def _():
        x = x_ref[...]
        ms = jnp.mean(x * x, axis=-1, keepdims=True)
        h_ref[...] = (x * lax.rsqrt(ms + EPS) * g_ref[...]).astype(BF16)

    o_ref[...] = _dot(h_ref[...], w_ref[...])


def _inproj(x2, g, w_pad, tm, tn):
    S, D = x2.shape
    N = w_pad.shape[1]
    return pl.pallas_call(
        _inproj_kernel,
        grid=(S // tm, N // tn),
        in_specs=[
            pl.BlockSpec((tm, D), lambda i, j: (i, 0)),
            pl.BlockSpec((1, D), lambda i, j: (0, 0)),
            pl.BlockSpec((D, tn), lambda i, j: (0, j)),
        ],
        out_specs=pl.BlockSpec((tm, tn), lambda i, j: (i, j)),
        out_shape=jax.ShapeDtypeStruct((S, N), F32),
        scratch_shapes=[pltpu.VMEM((tm, D), BF16)],
        compiler_params=pltpu.CompilerParams(
            dimension_semantics=("arbitrary", "arbitrary"), vmem_limit_bytes=VMEM_LIMIT),
        name="inproj",
    )(x2, g, w_pad)


def _lru_kernel(x_ref, g_ref, cw_ref, cb_ref, wa_ref, ba_ref, wx_ref, bx_ref, lam_ref,
                o_ref, xbuf_ref, h_ref):
    tm = x_ref.shape[0]

    @pl.when(pl.program_id(0) == 0)
    def _():
        xbuf_ref[0:8, :] = jnp.zeros((8, GROUP_W), F32)
        h_ref[...] = jnp.zeros_like(h_ref)

    x = x_ref[...]
    xbuf_ref[8:8 + tm, :] = x
    conv = cb_ref[...] + cw_ref[CONV_W - 1:CONV_W, :] * x
    for j in range(CONV_W - 1):
        off = 8 - (CONV_W - 1) + j
        conv = conv + cw_ref[j:j + 1, :] * xbuf_ref[off:off + tm, :]
    xbuf_ref[0:8, :] = x[tm - 8:tm, :]

    cb16 = conv.astype(BF16)
    r = jax.nn.sigmoid(_dot(cb16, wa_ref[...]) + ba_ref[...])
    ig = jax.nn.sigmoid(_dot(cb16, wx_ref[...]) + bx_ref[...])
    log_a = (-LRU_C) * r * _softplus(-lam_ref[...])
    a = jnp.exp(log_a)
    mult = jnp.sqrt(jnp.maximum(-jnp.tanh(log_a) * (a * a + 1.0), 1e-12))
    u = mult * (ig * conv)

    rows = _iota2((tm, GROUP_W), 0)
    d = 1
    while d < tm:
        keep = rows >= d
        a_sh = jnp.where(keep, pltpu.roll(a, d, 0), 1.0)
        u_sh = jnp.where(keep, pltpu.roll(u, d, 0), 0.0)
        u = u + a * u_sh
        a = a * a_sh
        d *= 2
    h = u + a * h_ref[...]
    h_ref[...] = h[tm - 1:tm, :]
    o_ref[...] = h * _silu(g_ref[...])


def _lru(p, cw, cb, wa_bd, ba, wx_bd, bx, lam, tm):
    S = p.shape[0]
    row = lambda n: pl.BlockSpec((n, GROUP_W), lambda i: (0, 0))
    return pl.pallas_call(
        _lru_kernel,
        grid=(S // tm,),
        in_specs=[
            pl.BlockSpec((tm, GROUP_W), lambda i: (i, COL_LRU_X // GROUP_W)),
            pl.BlockSpec((tm, GROUP_W), lambda i: (i, COL_LRU_G // GROUP_W)),
            row(CONV_W), row(1),
            pl.BlockSpec((GROUP_W, GROUP_W), lambda i: (0, 0)), row(1),
            pl.BlockSpec((GROUP_W, GROUP_W), lambda i: (0, 0)), row(1),
            row(1),
        ],
        out_specs=pl.BlockSpec((tm, GROUP_W), lambda i: (i, 0)),
        out_shape=jax.ShapeDtypeStruct((S, GROUP_W), F32),
        scratch_shapes=[pltpu.VMEM((tm + 8, GROUP_W), F32), pltpu.VMEM((1, GROUP_W), F32)],
        compiler_params=pltpu.CompilerParams(
            dimension_semantics=("arbitrary",), vmem_limit_bytes=VMEM_LIMIT),
        name="lru",
    )(p, p, cw, cb, wa_bd, ba, wx_bd, bx, lam)


def _stack_heads(x, lo):
    return jnp.concatenate([jnp.where(lo, x, 0.0), jnp.where(lo, 0.0, x)], axis=0)


def _rwkv_kernel(r_ref, k_ref, v_ref, l_ref, g_ref, mu_ref, w0_ref, wup_ref, a0_ref, aup_ref,
                 kk_ref, ka_ref, rk_ref, gng_ref, gnb_ref, o_ref,
                 prev_ref, state_ref, y_ref, *, passes):
    tm = r_ref.shape[0]
    n_chunks = tm // CHUNK
    n_pairs = GROUP_W // LANES
    WIN = 3 * GROUP_W + LANES

    @pl.when(pl.program_id(0) == 0)
    def _():
        prev_ref[...] = jnp.zeros_like(prev_ref)
        state_ref[...] = jnp.zeros_like(state_ref)

    row_t = _iota2((tm, 1), 0)

    def shift_lerp(ref, c0, c1):
        x = ref[...]
        xs = jnp.where(row_t == 0, prev_ref[:, c0:c1], pltpu.roll(x, 1, 0))
        prev_ref[:, c0:c1] = x[tm - 1:tm, :]
        return x + (xs - x) * mu_ref[:, c0:c1]

    r = shift_lerp(r_ref, 0, GROUP_W)
    k = shift_lerp(k_ref, GROUP_W, 2 * GROUP_W)
    v = shift_lerp(v_ref, 2 * GROUP_W, 3 * GROUP_W)
    lo_in = shift_lerp(l_ref, 3 * GROUP_W, WIN)

    lane_l = _iota2((tm, LANES), 1)
    lora = jnp.where(lane_l < LORA, jnp.tanh(lo_in), lo_in)
    w = -_softplus(-(w0_ref[...] + _mm(lora, wup_ref[...], passes=3))) - 0.5
    lw = -jnp.exp(w)
    a = jax.nn.sigmoid(a0_ref[...] + _mm(lora, aup_ref[...], passes=3))

    ones_bd = _head_ones()
    kk = k * kk_ref[...]
    kk = kk / jnp.maximum(jnp.sqrt(_segsum64(kk * kk, ones_bd)), 1e-12)
    k2 = k * (1.0 + (a - 1.0) * ka_ref[...])
    b = kk * a
    bonus = _segsum64(r * k2 * rk_ref[...], ones_bd) * v

    cl = lw
    rows_c = _iota2((tm, GROUP_W), 0) % CHUNK
    d = 1
    while d < CHUNK:
        cl = cl + jnp.where(rows_c >= d, pltpu.roll(cl, d, 0), 0.0)
        d *= 2

    ri = _iota2((LANES, LANES), 0)
    ci = _iota2((LANES, LANES), 1)
    same_head = (ri // CHUNK) == (ci // CHUNK)
    strict = same_head & (ci < ri)
    incl = same_head & (ci <= ri)
    diag16 = strict & ((ri // 16) == (ci // 16))
    off32 = strict & ((ri // 32) == (ci // 32)) & ((ri // 16) != (ci // 16))
    off64 = strict & ((ri // 32) != (ci // 32))
    eye = (ri == ci).astype(F32)
    lo_lane = _iota2((CHUNK, LANES), 1) < HEAD64

    mm = functools.partial(_mm, passes=passes)

    for c in range(n_chunks):
        r0, r1 = c * CHUNK, (c + 1) * CHUNK
        cl_c = cl[r0:r1, :]
        cl_end = cl_c[CHUNK - 1:CHUNK, :]
        e_in = jnp.exp(cl_c)
        e_ex = jnp.exp(cl_c - lw[r0:r1, :])
        e_inv = jnp.exp(-cl_c)
        e_end = jnp.exp(cl_end - cl_c)
        w_end = jnp.exp(cl_end)
        a_bar = -kk[r0:r1, :] * e_ex
        r_bar = r[r0:r1, :] * e_in
        b_bar = b[r0:r1, :] * e_inv
        k_bar = k2[r0:r1, :] * e_inv
        b_end = b[r0:r1, :] * e_end
        k_end = k2[r0:r1, :] * e_end
        v_c = v[r0:r1, :]

        for p in range(n_pairs):
            c0, c1 = p * LANES, (p + 1) * LANES
            st = lambda t: _stack_heads(t[:, c0:c1], lo_lane)
            a_s, r_s, b_s, k_s = st(a_bar), st(r_bar), st(b_bar), st(k_bar)
            be_s, ke_s, v_s = st(b_end), st(k_end), st(v_c)

            g = mm(jnp.concatenate([a_s, r_s], axis=0), jnp.concatenate([b_s, k_s], axis=0), NT)
            g_ab, g_ak = g[0:LANES, 0:LANES], g[0:LANES, LANES:2 * LANES]
            g_rb, g_rk = g[LANES:2 * LANES, 0:LANES], g[LANES:2 * LANES, LANES:2 * LANES]
            l_ak = jnp.where(strict, g_ak, 0.0)
            m_rb = jnp.where(incl, g_rb, 0.0)
            m_rk = jnp.where(incl, g_rk, 0.0)

            d1 = jnp.where(diag16, g_ab, 0.0)
            d2 = mm(d1, d1)
            d4 = mm(d2, d2)
            d8 = mm(d4, d4)
            t = eye + d1
            t = t + mm(t, d2)
            t = t + mm(t, d4)
            t = t + mm(t, d8)
            t = t + mm(t, mm(jnp.where(off32, g_ab, 0.0), t))
            t = t + mm(t, mm(jnp.where(off64, g_ab, 0.0), t))

            lv = mm(l_ak, v_s)
            tu = mm(t, jnp.concatenate([a_s, lv], axis=1))
            ru = mm(m_rb, tu)
            r_hat = r_s + ru[:, 0:LANES]
            y_v = ru[:, LANES:2 * LANES] + mm(m_rk, v_s)
            pq = mm(be_s, tu, TN)
            q = pq[:, LANES:2 * LANES] + mm(ke_s, v_s, TN)

            s0 = state_ref[p]
            y_s = mm(r_hat, s0) + y_v
            wcol = jnp.sum(eye * w_end[:, c0:c1], axis=1, keepdims=True)
            state_ref[p] = wcol * s0 + mm(pq[:, 0:LANES], s0) + q
            y_ref[r0:r1, c0:c1] = y_s[0:CHUNK, :] + y_s[CHUNK:2 * CHUNK, :]

    y = y_ref[...]
    inv_n = 1.0 / HEAD64
    mean = _segsum64(y, ones_bd) * inv_n
    yc = y - mean
    var = _segsum64(yc * yc, ones_bd) * inv_n
    yn = yc * lax.rsqrt(var + RWKV_GN_EPS) * gng_ref[...] + gnb_ref[...]
    o_ref[...] = (yn + bonus) * _silu(g_ref[...])


def _rwkv(p, mu, w0, wup, a0, aup, kkw, kaw, rkw, gng, gnb, tm, passes):
    S = p.shape[0]
    row = lambda n=1, w=GROUP_W: pl.BlockSpec((n, w), lambda i: (0, 0))
    col = lambda off, w=GROUP_W: pl.BlockSpec((tm, w), lambda i: (i, off // w))
    WIN = 3 * GROUP_W + LANES
    return pl.pallas_call(
        functools.partial(_rwkv_kernel, passes=passes),
        grid=(S // tm,),
        in_specs=[
            col(COL_RW_R), col(COL_RW_K), col(COL_RW_V), col(COL_RW_LORA, LANES), col(COL_RW_G),
            row(1, WIN), row(), row(LANES), row(), row(LANES),
            row(), row(), row(), row(), row(),
        ],
        out_specs=pl.BlockSpec((tm, GROUP_W), lambda i: (i, 0)),
        out_shape=jax.ShapeDtypeStruct((S, GROUP_W), F32),
        scratch_shapes=[
            pltpu.VMEM((1, WIN), F32),
            pltpu.VMEM((GROUP_W // LANES, LANES, LANES), F32),
            pltpu.VMEM((tm, GROUP_W), F32),
        ],
        compiler_params=pltpu.CompilerParams(
            dimension_semantics=("arbitrary",), vmem_limit_bytes=VMEM_LIMIT),
        name="rwkv7",
    )(p, p, p, p, p, mu, w0, wup, a0, aup, kkw, kaw, rkw, gng, gnb)


def _swa_kernel(sink_ref, q_ref, kv_ref, kvp_ref, g_ref, qg_ref, kg_ref, o_ref):
    tq = q_ref.shape[0]
    nblk = tq // WINDOW
    ones_bd = _head_ones()
    inv_d = 1.0 / HEAD64

    def norm(x, gain):
        return x * lax.rsqrt(_segsum64(x * x, ones_bd) * inv_d + EPS) * gain

    kv_all = jnp.concatenate([kvp_ref[...], kv_ref[...]], axis=0)
    k_all = norm(kv_all[:, 0:LANES], kg_ref[...])
    v_all = kv_all[:, LANES:2 * LANES]
    lo_v = _iota2((2 * WINDOW, LANES), 1) < HEAD64
    lo_q = _iota2((WINDOW, LANES), 1) < HEAD64

    qi = _iota2((2 * WINDOW, 2 * WINDOW), 0) % WINDOW
    kj = _iota2((2 * WINDOW, 2 * WINDOW), 1)
    band = (kj > qi) & (kj <= qi + WINDOW)
    band_first = band & ((kj >= WINDOW) | (pl.program_id(0) > 0))
    upper_rows = _iota2((2 * WINDOW, 1), 0) >= WINDOW
    scale = HEAD64 ** -0.5

    for blk in range(nblk):
        r0 = blk * WINDOW
        keys = k_all[r0:r0 + 2 * WINDOW, :]
        vals = v_all[r0:r0 + 2 * WINDOW, :]
        v_lo = jnp.where(lo_v, vals, 0.0)
        v_hi = jnp.where(lo_v, 0.0, vals)
        mask = band_first if blk == 0 else band
        for i in range(GROUP_W // LANES):
            c0, c1 = i * LANES, (i + 1) * LANES
            qn = norm(q_ref[r0:r0 + WINDOW, c0:c1], qg_ref[...])
            q_s = jnp.concatenate([jnp.where(lo_q, qn, 0.0), jnp.where(lo_q, 0.0, qn)], axis=0)
            s = _mm(q_s, keys, NT) * scale
            s = jnp.where(mask, s, NEG_BIG)
            sink = jnp.where(upper_rows, sink_ref[4 + i], sink_ref[i])
            m = jnp.maximum(jnp.max(s, axis=-1, keepdims=True), sink)
            e = jnp.exp(s - m)
            den = jnp.sum(e, axis=-1, keepdims=True) + jnp.exp(sink - m)
            pr = e / den
            out = _mm(pr[0:WINDOW, :], v_lo) + _mm(pr[WINDOW:2 * WINDOW, :], v_hi)
            o_ref[r0:r0 + WINDOW, c0:c1] = out * _silu(g_ref[r0:r0 + WINDOW, c0:c1])


def _swa(p, sinks_perm, qg2, kg2, tq):
    S = p.shape[0]
    nblk = tq // WINDOW
    return pl.pallas_call(
        _swa_kernel,
        grid=(S // tq,),
        in_specs=[
            pl.BlockSpec(memory_space=pltpu.SMEM),
            pl.BlockSpec((tq, GROUP_W), lambda i: (i, COL_SWA_Q // GROUP_W)),
            pl.BlockSpec((tq, 2 * LANES), lambda i: (i, COL_SWA_KV // (2 * LANES))),
            pl.BlockSpec((WINDOW, 2 * LANES),
                         lambda i: (jnp.maximum(i * nblk - 1, 0), COL_SWA_KV // (2 * LANES))),
            pl.BlockSpec((tq, GROUP_W), lambda i: (i, COL_SWA_G // GROUP_W)),
            pl.BlockSpec((1, LANES), lambda i: (0, 0)),
            pl.BlockSpec((1, LANES), lambda i: (0, 0)),
        ],
        out_specs=pl.BlockSpec((tq, GROUP_W), lambda i: (i, 0)),
        out_shape=jax.ShapeDtypeStruct((S, GROUP_W), F32),
        compiler_params=pltpu.CompilerParams(
            dimension_semantics=("arbitrary",), vmem_limit_bytes=VMEM_LIMIT),
        name="swa",
    )(sinks_perm, p, p, p, p, qg2, kg2)


def _memkv_kernel(m_ref, g_ref, w_ref, kg_ref, o_ref, h_ref):
    j = pl.program_id(0)

    @pl.when(j == 0)
    def _():
        x = m_ref[...]
        ms = jnp.mean(x * x, axis=-1, keepdims=True)
        h_ref[...] = (x * lax.rsqrt(ms + EPS) * g_ref[...]).astype(BF16)

    kv = _dot(h_ref[...], w_ref[...])

    @pl.when(j < MEM_HEADS)
    def _():
        ms = jnp.mean(kv * kv, axis=-1, keepdims=True)
        o_ref[...] = kv * lax.rsqrt(ms + EPS) * kg_ref[...]

    @pl.when(j >= MEM_HEADS)
    def _():
        o_ref[...] = kv


def _memkv(mem2, g, w_kv, kg):
    M, D = mem2.shape
    N = w_kv.shape[1]
    return pl.pallas_call(
        _memkv_kernel,
        grid=(N // LANES,),
        in_specs=[
            pl.BlockSpec((M, D), lambda j: (0, 0)),
            pl.BlockSpec((1, D), lambda j: (0, 0)),
            pl.BlockSpec((D, LANES), lambda j: (0, j)),
            pl.BlockSpec((1, LANES), lambda j: (0, 0)),
        ],
        out_specs=pl.BlockSpec((M, LANES), lambda j: (0, j)),
        out_shape=jax.ShapeDtypeStruct((M, N), F32),
        scratch_shapes=[pltpu.VMEM((M, D), BF16)],
        compiler_params=pltpu.CompilerParams(
            dimension_semantics=("arbitrary",), vmem_limit_bytes=VMEM_LIMIT),
        name="memkv",
    )(mem2, g, w_kv, kg)


def _memattn_kernel(q_ref, g_ref, kv_ref, qg_ref, o_ref):
    head = GROUP_W // MEM_HEADS
    scale = head ** -0.5
    for h in range(MEM_HEADS):
        c0, c1 = h * head, (h + 1) * head
        q = q_ref[:, c0:c1]
        qn = q * lax.rsqrt(jnp.mean(q * q, axis=-1, keepdims=True) + EPS) * qg_ref[...]
        s = _mm(qn, kv_ref[:, c0:c1], NT) * scale
        m = jnp.max(s, axis=-1, keepdims=True)
        e = jnp.exp(s - m)
        pr = e / jnp.sum(e, axis=-1, keepdims=True)
        out = _mm(pr, kv_ref[:, GROUP_W + c0:GROUP_W + c1])
        o_ref[:, c0:c1] = out * _silu(g_ref[:, c0:c1])


def _memattn(p, kv, qg, tq):
    S = p.shape[0]
    return pl.pallas_call(
        _memattn_kernel,
        grid=(S // tq,),
        in_specs=[
            pl.BlockSpec((tq, GROUP_W), lambda i: (i, COL_MEM_Q // GROUP_W)),
            pl.BlockSpec((tq, GROUP_W), lambda i: (i, COL_MEM_G // GROUP_W)),
            pl.BlockSpec((N_MEM, 2 * GROUP_W), lambda i: (0, 0)),
            pl.BlockSpec((1, GROUP_W // MEM_HEADS), lambda i: (0, 0)),
        ],
        out_specs=pl.BlockSpec((tq, GROUP_W), lambda i: (i, 0)),
        out_shape=jax.ShapeDtypeStruct((S, GROUP_W), F32),
        compiler_params=pltpu.CompilerParams(
            dimension_semantics=("arbitrary",), vmem_limit_bytes=VMEM_LIMIT),
        name="memattn",
    )(p, p, kv, qg)


def _outproj_kernel(x_ref, o1_ref, o2_ref, o3_ref, o4_ref, w_ref, y_ref):
    acc = x_ref[...]
    for n, o_ref in enumerate((o1_ref, o2_ref, o3_ref, o4_ref)):
        acc = acc + _dot(o_ref[...].astype(BF16), w_ref[n * GROUP_W:(n + 1) * GROUP_W, :])
    y_ref[...] = acc


def _outproj(x2, outs, w_out, tm):
    S, D = x2.shape
    o_spec = pl.BlockSpec((tm, GROUP_W), lambda i: (i, 0))
    return pl.pallas_call(
        _outproj_kernel,
        grid=(S // tm,),
        in_specs=[pl.BlockSpec((tm, D), lambda i: (i, 0)), o_spec, o_spec, o_spec, o_spec,
                  pl.BlockSpec((4 * GROUP_W, D), lambda i: (0, 0))],
        out_specs=pl.BlockSpec((tm, D), lambda i: (i, 0)),
        out_shape=jax.ShapeDtypeStruct((S, D), F32),
        compiler_params=pltpu.CompilerParams(
            dimension_semantics=("arbitrary",), vmem_limit_bytes=VMEM_LIMIT),
        name="outproj",
    )(x2, *outs, w_out)


SWA_PERM = (0, 4, 1, 5, 2, 6, 3, 7)


def _perm_heads(t, axis):
    parts = jnp.split(t, 8, axis=axis)
    return jnp.concatenate([parts[h] for h in SWA_PERM], axis=axis)


def _pad_in_weights(w):
    D = w.shape[0]
    sizes = (512, 512, 512, 512, 512, LORA, LORA, 512, 512, 128, 128, 512, 512, 512)
    offs = [0]
    for s in sizes:
        offs.append(offs[-1] + s)
    (lru_x, lru_g, rw_r, rw_k, rw_v, rw_wd, rw_ad, rw_g,
     sq, sk, sv, sg, mq, mg) = [w[:, offs[n]:offs[n + 1]] for n in range(len(sizes))]
    z = lambda n: jnp.zeros((D, n), w.dtype)
    cols = [lru_x, lru_g, rw_r, rw_k, rw_v, rw_g, _perm_heads(sq, 1), _perm_heads(sg, 1), mq, mg,
            sk, sv, rw_wd, rw_ad, z(LANES - 2 * LORA), z(LANES)]
    out = jnp.concatenate(cols, axis=1)
    assert out.shape[1] == IN_PAD
    return out.astype(BF16)


def _block_diag(w):
    nb, bw, _ = w.shape
    eye = jnp.eye(nb, dtype=w.dtype)
    return (eye[:, None, :, None] * w[:, :, None, :]).reshape(nb * bw, nb * bw)


def _layer(x2, mem2, prm, tiles, passes):
    (norm_g, w_in, conv_w, conv_b, lru_wa, lru_ba, lru_wx, lru_bx, lru_lambda,
     rw_mu, rw_w0, rw_w_up, rw_a0, rw_a_up, rw_k_k, rw_k_a, rw_r_k, rw_gn_g, rw_gn_b,
     swa_q_g, swa_k_g, swa_sinks, mem_norm_g, w_mem_kv, mem_q_g, mem_k_g, w_out) = prm
    row = lambda t: t.reshape(1, -1)

    p = _inproj(x2, row(norm_g), _pad_in_weights(w_in), tiles["inproj_m"], tiles["inproj_n"])

    o_lru = _lru(p, conv_w, row(conv_b), _block_diag(lru_wa).astype(BF16), row(lru_ba),
                 _block_diag(lru_wx).astype(BF16), row(lru_bx), row(lru_lambda), tiles["lru"])

    mu_pad = jnp.concatenate([rw_mu, jnp.zeros((LANES - 2 * LORA,), F32)]).reshape(1, -1)
    zpad = jnp.zeros((LANES - LORA, GROUP_W), F32)
    wup_pad = jnp.concatenate([rw_w_up, zpad], axis=0)
    aup_pad = jnp.concatenate([jnp.zeros((LORA, GROUP_W), F32), rw_a_up,
                               jnp.zeros((LANES - 2 * LORA, GROUP_W), F32)], axis=0)
    o_rw = _rwkv(p, mu_pad, row(rw_w0), wup_pad, row(rw_a0), aup_pad, row(rw_k_k), row(rw_k_a),
                 row(rw_r_k), row(rw_gn_g), row(rw_gn_b), tiles["rwkv"], passes)

    two = lambda t: jnp.concatenate([t, t]).reshape(1, LANES)
    o_swa = _swa(p, swa_sinks, two(swa_q_g), two(swa_k_g), tiles["swa"])

    kv = _memkv(mem2, row(mem_norm_g), w_mem_kv.astype(BF16), row(mem_k_g))
    o_mem = _memattn(p, kv, row(mem_q_g), tiles["mem"])

    w_out_p = jnp.concatenate(
        [w_out[0:2 * GROUP_W], _perm_heads(w_out[2 * GROUP_W:3 * GROUP_W], 0), w_out[3 * GROUP_W:]],
        axis=0).astype(BF16)
    return _outproj(x2, (o_lru, o_rw, o_swa, o_mem), w_out_p, tiles["outproj"])


def _tiles(S):
    pick = lambda t: t if S % t == 0 else S
    return {"inproj_m": pick(1024), "inproj_n": 512, "lru": pick(512), "rwkv": pick(256),
            "swa": pick(512), "mem": pick(512), "outproj": pick(512)}


def kernel(x, mem, norm_g, w_in, conv_w, conv_b, lru_wa, lru_ba, lru_wx, lru_bx, lru_lambda, rw_mu, rw_w0, rw_w_up, rw_a0, rw_a_up, rw_k_k, rw_k_a, rw_r_k, rw_gn_g, rw_gn_b, swa_q_g, swa_k_g, swa_sinks, mem_norm_g, w_mem_kv, mem_q_g, mem_k_g, w_out):
    B, S, D = x.shape
    assert B == 1
    params = (norm_g, w_in, conv_w, conv_b, lru_wa, lru_ba, lru_wx, lru_bx, lru_lambda,
              rw_mu, rw_w0, rw_w_up, rw_a0, rw_a_up, rw_k_k, rw_k_a, rw_r_k, rw_gn_g, rw_gn_b,
              swa_q_g, swa_k_g, swa_sinks, mem_norm_g, w_mem_kv, mem_q_g, mem_k_g, w_out)
    x2 = x.reshape(S, D)
    mem2 = mem.reshape(mem.shape[1], D)
    tiles = _tiles(S)
    for l in range(norm_g.shape[0]):
        x2 = _layer(x2, mem2, tuple(t[l] for t in params), tiles, passes=1)
    return x2.reshape(B, S, D)
```

```python
import functools

import jax
import jax.numpy as jnp
from jax import lax
from jax.experimental import pallas as pl
from jax.experimental.pallas import tpu as pltpu

F32 = jnp.float32
BF16 = jnp.bfloat16

LANES = 128
GROUP_W = 512
HEAD64 = 64
CHUNK = 64
CONV_W = 4
LRU_C = 8.0
LORA = 32
EPS = 1e-6
RWKV_GN_EPS = 64e-5
WINDOW = 128
N_MEM = 256
MEM_HEADS = 4
NEG_BIG = -1e30
VMEM_LIMIT = 56 * 1024 * 1024

COL_LRU_X, COL_LRU_G = 0, 512
COL_RW_R, COL_RW_K, COL_RW_V, COL_RW_G = 1024, 1536, 2048, 2560
COL_SWA_Q, COL_SWA_G = 3072, 3584
COL_MEM_Q, COL_MEM_G = 4096, 4608
COL_SWA_KV, COL_RW_LORA = 5120, 5376
IN_PAD = 5632

NN = (((1,), (0,)), ((), ()))
NT = (((1,), (1,)), ((), ()))
TN = (((0,), (0,)), ((), ()))
BNN = (((2,), (1,)), ((0,), (0,)))
BNT = (((2,), (2,)), ((0,), (0,)))
BTN = (((1,), (1,)), ((0,), (0,)))


def _dot(a, b, dims=NN):
    return lax.dot_general(a, b, dims, preferred_element_type=F32)


def _split(a):
    hi = a.astype(BF16)
    lo = (a - hi.astype(F32)).astype(BF16)
    return hi, lo


def _mm(a, b, dims=NN, passes=1):
    if passes == 1:
        return _dot(a.astype(BF16), b.astype(BF16), dims)
    ah, al = _split(a)
    bh, bl = _split(b)
    return _dot(ah, bh, dims) + (_dot(ah, bl, dims) + _dot(al, bh, dims))


def _mm_exact_rhs(a, b_bf16):
    ah, al = _split(a)
    return _dot(ah, b_bf16) + _dot(al, b_bf16)


def _iota2(shape, axis):
    return lax.broadcasted_iota(jnp.int32, shape, axis)


def _head_ones():
    r = _iota2((LANES, LANES), 0) // HEAD64
    c = _iota2((LANES, LANES), 1) // HEAD64
    return (r == c).astype(BF16)


def _segsum64(x, ones_bd):
    n, w = x.shape
    cols = [_mm_exact_rhs(x[:, j:j + LANES], ones_bd) for j in range(0, w, LANES)]
    return cols[0] if len(cols) == 1 else jnp.concatenate(cols, axis=1)


def _softplus(z):
    return jnp.maximum(z, 0.0) + jnp.log1p(jnp.exp(-jnp.abs(z)))


def _silu(z):
    return z * jax.nn.sigmoid(z)


def _inproj_kernel(x_ref, g_ref, w_ref, o_ref, h_ref):
    @pl.when(pl.program_id(1) == 0)
    def _():
        x = x_ref[...]
        ms = jnp.mean(x * x, axis=-1, keepdims=True)
        h_ref[...] = (x * lax.rsqrt(ms + EPS) * g_ref[...]).astype(BF16)

    o_ref[...] = _dot(h_ref[...], w_ref[...])


def _inproj(x2, g, w_pad, tm, tn):
    S, D = x2.shape
    N = w_pad.shape[1]
    return pl.pallas_call(
        _inproj_kernel,
        grid=(S // tm, N // tn),
        in_specs=[
            pl.BlockSpec((tm, D), lambda i, j: (i, 0)),
            pl.BlockSpec((1, D), lambda i, j: (0, 0)),
            pl.BlockSpec((D, tn), lambda i, j: (0, j)),
        ],
        out_specs=pl.BlockSpec((tm, tn), lambda i, j: (i, j)),
        out_shape=jax.ShapeDtypeStruct((S, N), F32),
        scratch_shapes=[pltpu.VMEM((tm, D), BF16)],
        compiler_params=pltpu.CompilerParams(
            dimension_semantics=("arbitrary", "arbitrary"), vmem_limit_bytes=VMEM_LIMIT),
        name="inproj",
    )(x2, g, w_pad)


def _lru_kernel(x_ref, g_ref, cw_ref, cb_ref, wa_ref, ba_ref, wx_ref, bx_ref, lam_ref,
                o_ref, xbuf_ref, h_ref):
    tm = x_ref.shape[0]

    @pl.when(pl.program_id(0) == 0)
    def _():
        xbuf_ref[0:8, :] = jnp.zeros((8, GROUP_W), F32)
        h_ref[...] = jnp.zeros_like(h_ref)

    x = x_ref[...]
    xbuf_ref[8:8 + tm, :] = x
    conv = cb_ref[...] + cw_ref[CONV_W - 1:CONV_W, :] * x
    for j in range(CONV_W - 1):
        off = 8 - (CONV_W - 1) + j
        conv = conv + cw_ref[j:j + 1, :] * xbuf_ref[off:off + tm, :]
    xbuf_ref[0:8, :] = x[tm - 8:tm, :]

    cb16 = conv.astype(BF16)
    r = jax.nn.sigmoid(_dot(cb16, wa_ref[...]) + ba_ref[...])
    ig = jax.nn.sigmoid(_dot(cb16, wx_ref[...]) + bx_ref[...])
    log_a = (-LRU_C) * r * _softplus(-lam_ref[...])
    a = jnp.exp(log_a)
    mult = jnp.sqrt(jnp.maximum(-jnp.tanh(log_a) * (a * a + 1.0), 1e-12))
    u = mult * (ig * conv)

    rows = _iota2((tm, GROUP_W), 0)
    d = 1
    while d < tm:
        keep = rows >= d
        a_sh = jnp.where(keep, pltpu.roll(a, d, 0), 1.0)
        u_sh = jnp.where(keep, pltpu.roll(u, d, 0), 0.0)
        u = u + a * u_sh
        a = a * a_sh
        d *= 2
    h = u + a * h_ref[...]
    h_ref[...] = h[tm - 1:tm, :]
    o_ref[...] = h * _silu(g_ref[...])


def _lru(p, cw, cb, wa_bd, ba, wx_bd, bx, lam, tm):
    S = p.shape[0]
    row = lambda n: pl.BlockSpec((n, GROUP_W), lambda i: (0, 0))
    return pl.pallas_call(
        _lru_kernel,
        grid=(S // tm,),
        in_specs=[
            pl.BlockSpec((tm, GROUP_W), lambda i: (i, COL_LRU_X // GROUP_W)),
            pl.BlockSpec((tm, GROUP_W), lambda i: (i, COL_LRU_G // GROUP_W)),
            row(CONV_W), row(1),
            pl.BlockSpec((GROUP_W, GROUP_W), lambda i: (0, 0)), row(1),
            pl.BlockSpec((GROUP_W, GROUP_W), lambda i: (0, 0)), row(1),
            row(1),
        ],
        out_specs=pl.BlockSpec((tm, GROUP_W), lambda i: (i, 0)),
        out_shape=jax.ShapeDtypeStruct((S, GROUP_W), F32),
        scratch_shapes=[pltpu.VMEM((tm + 8, GROUP_W), F32), pltpu.VMEM((1, GROUP_W), F32)],
        compiler_params=pltpu.CompilerParams(
            dimension_semantics=("arbitrary",), vmem_limit_bytes=VMEM_LIMIT),
        name="lru",
    )(p, p, cw, cb, wa_bd, ba, wx_bd, bx, lam)


def _rwkv_kernel(r_ref, k_ref, v_ref, l_ref, g_ref, mu_ref, w0_ref, wup_ref, a0_ref, aup_ref,
                 kk_ref, ka_ref, rk_ref, gng_ref, gnb_ref, o_ref,
                 prev_ref, state_ref, *, passes):
    tm = r_ref.shape[0]
    n_chunks = tm // CHUNK
    n_pairs = GROUP_W // LANES
    WIN = 3 * GROUP_W + LANES

    @pl.when(pl.program_id(0) == 0)
    def _():
        prev_ref[...] = jnp.zeros_like(prev_ref)
        state_ref[...] = jnp.zeros_like(state_ref)

    row_t = _iota2((tm, 1), 0)

    def shift_lerp(ref, c0, c1):
        x = ref[...]
        xs = jnp.where(row_t == 0, prev_ref[:, c0:c1], pltpu.roll(x, 1, 0))
        prev_ref[:, c0:c1] = x[tm - 1:tm, :]
        return x + (xs - x) * mu_ref[:, c0:c1]

    r = shift_lerp(r_ref, 0, GROUP_W)
    k = shift_lerp(k_ref, GROUP_W, 2 * GROUP_W)
    v = shift_lerp(v_ref, 2 * GROUP_W, 3 * GROUP_W)
    lo_in = shift_lerp(l_ref, 3 * GROUP_W, WIN)

    lane_l = _iota2((tm, LANES), 1)
    lora = jnp.where(lane_l < LORA, jnp.tanh(lo_in), lo_in)
    w = -_softplus(-(w0_ref[...] + _mm(lora, wup_ref[...], passes=3))) - 0.5
    lw = -jnp.exp(w)
    a = jax.nn.sigmoid(a0_ref[...] + _mm(lora, aup_ref[...], passes=3))

    ones_bd = _head_ones()
    kk = k * kk_ref[...]
    kk = kk / jnp.maximum(jnp.sqrt(_segsum64(kk * kk, ones_bd)), 1e-12)
    k2 = k * (1.0 + (a - 1.0) * ka_ref[...])
    b = kk * a
    bonus = _segsum64(r * k2 * rk_ref[...], ones_bd) * v

    cl = lw
    rows_c = _iota2((tm, GROUP_W), 0) % CHUNK
    d = 1
    while d < CHUNK:
        cl = cl + jnp.where(rows_c >= d, pltpu.roll(cl, d, 0), 0.0)
        d *= 2

    to3 = lambda t: t.reshape(n_chunks, CHUNK, GROUP_W)
    cl3 = to3(cl)
    cl_end = cl3[:, CHUNK - 1:CHUNK, :]
    e_inv = jnp.exp(-cl3)
    e_end = jnp.exp(cl_end - cl3)
    w_end = jnp.exp(cl_end)
    kk3, b3, k23 = to3(kk), to3(b), to3(k2)

    lo_lane = lax.broadcasted_iota(jnp.int32, (1, CHUNK, LANES), 2) < HEAD64
    n_units = n_chunks * n_pairs

    def stack(t3, dtype):
        units = []
        for p in range(n_pairs):
            x = t3[:, :, p * LANES:(p + 1) * LANES]
            units.append(jnp.concatenate(
                [jnp.where(lo_lane, x, 0.0), jnp.where(lo_lane, 0.0, x)], axis=1).astype(dtype))
        return jnp.stack(units, axis=1).reshape(n_units, 2 * CHUNK, LANES)

    op_dtype = BF16 if passes == 1 else F32
    cast = lambda t: t.astype(op_dtype)
    mm = functools.partial(_mm, passes=passes)

    a_s = stack(-kk3 * jnp.exp(cl3 - to3(lw)), op_dtype)
    r_s = stack(to3(r) * jnp.exp(cl3), F32)
    b_s = stack(b3 * e_inv, op_dtype)
    k_s = stack(k23 * e_inv, op_dtype)
    be_s = stack(b3 * e_end, op_dtype)
    ke_s = stack(k23 * e_end, op_dtype)
    v_s = stack(to3(v), op_dtype)
    wend_u = jnp.stack([w_end[:, :, p * LANES:(p + 1) * LANES] for p in range(n_pairs)],
                       axis=1).reshape(n_units, 1, LANES)

    ri = lax.broadcasted_iota(jnp.int32, (1, LANES, LANES), 1)
    ci = lax.broadcasted_iota(jnp.int32, (1, LANES, LANES), 2)
    same_head = (ri // CHUNK) == (ci // CHUNK)
    strict = same_head & (ci < ri)
    incl = same_head & (ci <= ri)
    diag16 = strict & ((ri // 16) == (ci // 16))
    off32 = strict & ((ri // 32) == (ci // 32)) & ((ri // 16) != (ci // 16))
    off64 = strict & ((ri // 32) != (ci // 32))
    eye = (ri == ci).astype(F32)

    g = mm(jnp.concatenate([a_s, cast(r_s)], axis=1), jnp.concatenate([b_s, k_s], axis=1), BNT)
    g_ab, g_ak = g[:, 0:LANES, 0:LANES], g[:, 0:LANES, LANES:2 * LANES]
    g_rb, g_rk = g[:, LANES:2 * LANES, 0:LANES], g[:, LANES:2 * LANES, LANES:2 * LANES]
    l_ak = cast(jnp.where(strict, g_ak, 0.0))
    m_rb = cast(jnp.where(incl, g_rb, 0.0))
    m_rk = cast(jnp.where(incl, g_rk, 0.0))

    d1 = jnp.where(diag16, g_ab, 0.0)
    d1c = cast(d1)
    d2 = cast(mm(d1c, d1c, BNN))
    d4 = cast(mm(d2, d2, BNN))
    d8 = mm(d4, d4, BNN)
    t = eye + d1
    t = t + mm(cast(t), d2, BNN)
    t = t + mm(cast(t), d4, BNN)
    t = t + mm(cast(t), d8, BNN)
    tc = cast(t)
    t = t + mm(tc, mm(cast(jnp.where(off32, g_ab, 0.0)), tc, BNN), BNN)
    tc = cast(t)
    t = t + mm(tc, mm(cast(jnp.where(off64, g_ab, 0.0)), tc, BNN), BNN)

    lv = mm(l_ak, v_s, BNN)
    tu = cast(mm(cast(t), jnp.concatenate([a_s, cast(lv)], axis=2), BNN))
    ru = mm(m_rb, tu, BNN)
    r_hat = cast(r_s + ru[:, :, 0:LANES])
    y_v = ru[:, :, LANES:2 * LANES] + mm(m_rk, v_s, BNN)
    pq = mm(be_s, tu, BTN)
    pp = cast(pq[:, :, 0:LANES])
    q = pq[:, :, LANES:2 * LANES] + mm(ke_s, v_s, BTN)
    wcol = jnp.sum(eye * wend_u, axis=2, keepdims=True)

    s0 = state_ref[...]
    y_parts = []
    for c in range(n_chunks):
        u0, u1 = c * n_pairs, (c + 1) * n_pairs
        s0c = cast(s0)
        y_parts.append(mm(r_hat[u0:u1], s0c, BNN) + y_v[u0:u1])
        s0 = wcol[u0:u1] * s0 + mm(pp[u0:u1], s0c, BNN) + q[u0:u1]
    state_ref[...] = s0
    y_s = jnp.concatenate(y_parts, axis=0)
    y4 = (y_s[:, 0:CHUNK, :] + y_s[:, CHUNK:2 * CHUNK, :]).reshape(n_chunks, n_pairs, CHUNK, LANES)
    y = jnp.concatenate([y4[:, p] for p in range(n_pairs)], axis=-1).reshape(tm, GROUP_W)
    inv_n = 1.0 / HEAD64
    mean = _segsum64(y, ones_bd) * inv_n
    yc = y - mean
    var = _segsum64(yc * yc, ones_bd) * inv_n
    yn = yc * lax.rsqrt(var + RWKV_GN_EPS) * gng_ref[...] + gnb_ref[...]
    o_ref[...] = (yn + bonus) * _silu(g_ref[...])


def _rwkv(p, mu, w0, wup, a0, aup, kkw, kaw, rkw, gng, gnb, tm, passes):
    S = p.shape[0]
    row = lambda n=1, w=GROUP_W: pl.BlockSpec((n, w), lambda i: (0, 0))
    col = lambda off, w=GROUP_W: pl.BlockSpec((tm, w), lambda i: (i, off // w))
    WIN = 3 * GROUP_W + LANES
    return pl.pallas_call(
        functools.partial(_rwkv_kernel, passes=passes),
        grid=(S // tm,),
        in_specs=[
            col(COL_RW_R), col(COL_RW_K), col(COL_RW_V), col(COL_RW_LORA, LANES), col(COL_RW_G),
            row(1, WIN), row(), row(LANES), row(), row(LANES),
            row(), row(), row(), row(), row(),
        ],
        out_specs=pl.BlockSpec((tm, GROUP_W), lambda i: (i, 0)),
        out_shape=jax.ShapeDtypeStruct((S, GROUP_W), F32),
        scratch_shapes=[
            pltpu.VMEM((1, WIN), F32),
            pltpu.VMEM((GROUP_W // LANES, LANES, LANES), F32),
        ],
        compiler_params=pltpu.CompilerParams(
            dimension_semantics=("arbitrary",), vmem_limit_bytes=VMEM_LIMIT),
        name="rwkv7",
    )(p, p, p, p, p, mu, w0, wup, a0, aup, kkw, kaw, rkw, gng, gnb)


def _swa_kernel(sink_ref, q_ref, kv_ref, kvp_ref, g_ref, qg_ref, kg_ref, o_ref):
    tq = q_ref.shape[0]
    nblk = tq // WINDOW
    ones_bd = _head_ones()
    inv_d = 1.0 / HEAD64

    def norm(x, gain):
        return x * lax.rsqrt(_segsum64(x * x, ones_bd) * inv_d + EPS) * gain

    kv_all = jnp.concatenate([kvp_ref[...], kv_ref[...]], axis=0)
    k_all = norm(kv_all[:, 0:LANES], kg_ref[...]).astype(BF16)
    v_all = kv_all[:, LANES:2 * LANES]
    lo_v = _iota2((WINDOW + tq, LANES), 1) < HEAD64
    v_lo_all = jnp.where(lo_v, v_all, 0.0).astype(BF16)
    v_hi_all = jnp.where(lo_v, 0.0, v_all).astype(BF16)

    window = lambda t: jnp.stack(
        [t[b * WINDOW:(b + 2) * WINDOW, :] for b in range(nblk)], axis=0)
    keys, v_lo, v_hi = window(k_all), window(v_lo_all), window(v_hi_all)

    shape3 = (nblk, 2 * WINDOW, 2 * WINDOW)
    qi = lax.broadcasted_iota(jnp.int32, shape3, 1) % WINDOW
    kj = lax.broadcasted_iota(jnp.int32, shape3, 2)
    blk = lax.broadcasted_iota(jnp.int32, shape3, 0)
    has_prev = (kj >= WINDOW) | (blk > 0) | (pl.program_id(0) > 0)
    mask = (kj > qi) & (kj <= qi + WINDOW) & has_prev
    upper_rows = lax.broadcasted_iota(jnp.int32, (1, 2 * WINDOW, 1), 1) >= WINDOW
    lo_q = lax.broadcasted_iota(jnp.int32, (1, WINDOW, LANES), 2) < HEAD64
    scale = HEAD64 ** -0.5

    for i in range(GROUP_W // LANES):
        c0, c1 = i * LANES, (i + 1) * LANES
        qn = (norm(q_ref[:, c0:c1], qg_ref[...]) * scale).reshape(nblk, WINDOW, LANES)
        q_s = jnp.concatenate([jnp.where(lo_q, qn, 0.0), jnp.where(lo_q, 0.0, qn)],
                              axis=1).astype(BF16)
        s = jnp.where(mask, _dot(q_s, keys, BNT), NEG_BIG)
        sink = jnp.where(upper_rows, sink_ref[4 + i], sink_ref[i])
        m = jnp.maximum(jnp.max(s, axis=-1, keepdims=True), sink)
        e = jnp.exp(s - m)
        rden = 1.0 / (jnp.sum(e, axis=-1, keepdims=True) + jnp.exp(sink - m))
        e = e.astype(BF16)
        out = (_dot(e[:, 0:WINDOW, :], v_lo, BNN) * rden[:, 0:WINDOW, :]
               + _dot(e[:, WINDOW:2 * WINDOW, :], v_hi, BNN) * rden[:, WINDOW:2 * WINDOW, :])
        o_ref[:, c0:c1] = out.reshape(tq, LANES) * _silu(g_ref[:, c0:c1])


def _swa(p, sinks_perm, qg2, kg2, tq):
    S = p.shape[0]
    nblk = tq // WINDOW
    return pl.pallas_call(
        _swa_kernel,
        grid=(S // tq,),
        in_specs=[
            pl.BlockSpec(memory_space=pltpu.SMEM),
            pl.BlockSpec((tq, GROUP_W), lambda i: (i, COL_SWA_Q // GROUP_W)),
            pl.BlockSpec((tq, 2 * LANES), lambda i: (i, COL_SWA_KV // (2 * LANES))),
            pl.BlockSpec((WINDOW, 2 * LANES),
                         lambda i: (jnp.maximum(i * nblk - 1, 0), COL_SWA_KV // (2 * LANES))),
            pl.BlockSpec((tq, GROUP_W), lambda i: (i, COL_SWA_G // GROUP_W)),
            pl.BlockSpec((1, LANES), lambda i: (0, 0)),
            pl.BlockSpec((1, LANES), lambda i: (0, 0)),
        ],
        out_specs=pl.BlockSpec((tq, GROUP_W), lambda i: (i, 0)),
        out_shape=jax.ShapeDtypeStruct((S, GROUP_W), F32),
        compiler_params=pltpu.CompilerParams(
            dimension_semantics=("arbitrary",), vmem_limit_bytes=VMEM_LIMIT),
        name="swa",
    )(sinks_perm, p, p, p, p, qg2, kg2)


def _memkv_kernel(m_ref, g_ref, w_ref, kg_ref, o_ref, h_ref):
    j = pl.program_id(0)

    @pl.when(j == 0)
    def _():
        x = m_ref[...]
        ms = jnp.mean(x * x, axis=-1, keepdims=True)
        h_ref[...] = (x * lax.rsqrt(ms + EPS) * g_ref[...]).astype(BF16)

    kv = _dot(h_ref[...], w_ref[...])

    @pl.when(j < MEM_HEADS)
    def _():
        ms = jnp.mean(kv * kv, axis=-1, keepdims=True)
        o_ref[...] = kv * lax.rsqrt(ms + EPS) * kg_ref[...]

    @pl.when(j >= MEM_HEADS)
    def _():
        o_ref[...] = kv


def _memkv(mem2, g, w_kv, kg):
    M, D = mem2.shape
    N = w_kv.shape[1]
    return pl.pallas_call(
        _memkv_kernel,
        grid=(N // LANES,),
        in_specs=[
            pl.BlockSpec((M, D), lambda j: (0, 0)),
            pl.BlockSpec((1, D), lambda j: (0, 0)),
            pl.BlockSpec((D, LANES), lambda j: (0, j)),
            pl.BlockSpec((1, LANES), lambda j: (0, 0)),
        ],
        out_specs=pl.BlockSpec((M, LANES), lambda j: (0, j)),
        out_shape=jax.ShapeDtypeStruct((M, N), F32),
        scratch_shapes=[pltpu.VMEM((M, D), BF16)],
        compiler_params=pltpu.CompilerParams(
            dimension_semantics=("arbitrary",), vmem_limit_bytes=VMEM_LIMIT),
        name="memkv",
    )(mem2, g, w_kv, kg)


def _memattn_kernel(q_ref, g_ref, kv_ref, qg_ref, o_ref):
    head = GROUP_W // MEM_HEADS
    scale = head ** -0.5
    for h in range(MEM_HEADS):
        c0, c1 = h * head, (h + 1) * head
        q = q_ref[:, c0:c1]
        qn = q * lax.rsqrt(jnp.mean(q * q, axis=-1, keepdims=True) + EPS) * qg_ref[...]
        s = _mm(qn, kv_ref[:, c0:c1], NT) * scale
        m = jnp.max(s, axis=-1, keepdims=True)
        e = jnp.exp(s - m)
        pr = e / jnp.sum(e, axis=-1, keepdims=True)
        out = _mm(pr, kv_ref[:, GROUP_W + c0:GROUP_W + c1])
        o_ref[:, c0:c1] = out * _silu(g_ref[:, c0:c1])


def _memattn(p, kv, qg, tq):
    S = p.shape[0]
    return pl.pallas_call(
        _memattn_kernel,
        grid=(S // tq,),
        in_specs=[
            pl.BlockSpec((tq, GROUP_W), lambda i: (i, COL_MEM_Q // GROUP_W)),
            pl.BlockSpec((tq, GROUP_W), lambda i: (i, COL_MEM_G // GROUP_W)),
            pl.BlockSpec((N_MEM, 2 * GROUP_W), lambda i: (0, 0)),
            pl.BlockSpec((1, GROUP_W // MEM_HEADS), lambda i: (0, 0)),
        ],
        out_specs=pl.BlockSpec((tq, GROUP_W), lambda i: (i, 0)),
        out_shape=jax.ShapeDtypeStruct((S, GROUP_W), F32),
        compiler_params=pltpu.CompilerParams(
            dimension_semantics=("arbitrary",), vmem_limit_bytes=VMEM_LIMIT),
        name="memattn",
    )(p, p, kv, qg)


def _outproj_kernel(x_ref, o1_ref, o2_ref, o3_ref, o4_ref, w_ref, y_ref):
    acc = x_ref[...]
    for n, o_ref in enumerate((o1_ref, o2_ref, o3_ref, o4_ref)):
        acc = acc + _dot(o_ref[...].astype(BF16), w_ref[n * GROUP_W:(n + 1) * GROUP_W, :])
    y_ref[...] = acc


def _outproj(x2, outs, w_out, tm):
    S, D = x2.shape
    o_spec = pl.BlockSpec((tm, GROUP_W), lambda i: (i, 0))
    return pl.pallas_call(
        _outproj_kernel,
        grid=(S // tm,),
        in_specs=[pl.BlockSpec((tm, D), lambda i: (i, 0)), o_spec, o_spec, o_spec, o_spec,
                  pl.BlockSpec((4 * GROUP_W, D), lambda i: (0, 0))],
        out_specs=pl.BlockSpec((tm, D), lambda i: (i, 0)),
        out_shape=jax.ShapeDtypeStruct((S, D), F32),
        compiler_params=pltpu.CompilerParams(
            dimension_semantics=("arbitrary",), vmem_limit_bytes=VMEM_LIMIT),
        name="outproj",
    )(x2, *outs, w_out)


SWA_PERM = (0, 4, 1, 5, 2, 6, 3, 7)


def _perm_heads(t, axis):
    parts = jnp.split(t, 8, axis=axis)
    return jnp.concatenate([parts[h] for h in SWA_PERM], axis=axis)


def _pad_in_weights(w):
    D = w.shape[0]
    w = w.astype(BF16)
    sizes = (512, 512, 512, 512, 512, LORA, LORA, 512, 512, 128, 128, 512, 512, 512)
    offs = [0]
    for s in sizes:
        offs.append(offs[-1] + s)
    (lru_x, lru_g, rw_r, rw_k, rw_v, rw_wd, rw_ad, rw_g,
     sq, sk, sv, sg, mq, mg) = [w[:, offs[n]:offs[n + 1]] for n in range(len(sizes))]
    z = lambda n: jnp.zeros((D, n), w.dtype)
    cols = [lru_x, lru_g, rw_r, rw_k, rw_v, rw_g, _perm_heads(sq, 1), _perm_heads(sg, 1), mq, mg,
            sk, sv, rw_wd, rw_ad, z(LANES - 2 * LORA), z(LANES)]
    out = jnp.concatenate(cols, axis=1)
    assert out.shape[1] == IN_PAD
    return out


def _block_diag(w):
    nb, bw, _ = w.shape
    eye = jnp.eye(nb, dtype=w.dtype)
    return (eye[:, None, :, None] * w[:, :, None, :]).reshape(nb * bw, nb * bw)


def _layer(x2, mem2, prm, tiles, passes):
    (norm_g, w_in, conv_w, conv_b, lru_wa, lru_ba, lru_wx, lru_bx, lru_lambda,
     rw_mu, rw_w0, rw_w_up, rw_a0, rw_a_up, rw_k_k, rw_k_a, rw_r_k, rw_gn_g, rw_gn_b,
     swa_q_g, swa_k_g, swa_sinks, mem_norm_g, w_mem_kv, mem_q_g, mem_k_g, w_out) = prm
    row = lambda t: t.reshape(1, -1)

    p = _inproj(x2, row(norm_g), _pad_in_weights(w_in), tiles["inproj_m"], tiles["inproj_n"])

    o_lru = _lru(p, conv_w, row(conv_b), _block_diag(lru_wa).astype(BF16), row(lru_ba),
                 _block_diag(lru_wx).astype(BF16), row(lru_bx), row(lru_lambda), tiles["lru"])

    mu_pad = jnp.concatenate([rw_mu, jnp.zeros((LANES - 2 * LORA,), F32)]).reshape(1, -1)
    zpad = jnp.zeros((LANES - LORA, GROUP_W), F32)
    wup_pad = jnp.concatenate([rw_w_up, zpad], axis=0)
    aup_pad = jnp.concatenate([jnp.zeros((LORA, GROUP_W), F32), rw_a_up,
                               jnp.zeros((LANES - 2 * LORA, GROUP_W), F32)], axis=0)
    o_rw = _rwkv(p, mu_pad, row(rw_w0), wup_pad, row(rw_a0), aup_pad, row(rw_k_k), row(rw_k_a),
                 row(rw_r_k), row(rw_gn_g), row(rw_gn_b), tiles["rwkv"], passes)

    two = lambda t: jnp.concatenate([t, t]).reshape(1, LANES)
    o_swa = _swa(p, swa_sinks, two(swa_q_g), two(swa_k_g), tiles["swa"])

    kv = _memkv(mem2, row(mem_norm_g), w_mem_kv.astype(BF16), row(mem_k_g))
    o_mem = _memattn(p, kv, row(mem_q_g), tiles["mem"])

    w_out_p = jnp.concatenate(
        [w_out[0:2 * GROUP_W], _perm_heads(w_out[2 * GROUP_W:3 * GROUP_W], 0), w_out[3 * GROUP_W:]],
        axis=0).astype(BF16)
    return _outproj(x2, (o_lru, o_rw, o_swa, o_mem), w_out_p, tiles["outproj"])


def _tiles(S):
    pick = lambda t: t if S % t == 0 else S
    return {"inproj_m": pick(1024), "inproj_n": 512, "lru": pick(512), "rwkv": pick(256),
            "swa": pick(1024), "mem": pick(512), "outproj": pick(512)}


def kernel(x, mem, norm_g, w_in, conv_w, conv_b, lru_wa, lru_ba, lru_wx, lru_bx, lru_lambda, rw_mu, rw_w0, rw_w_up, rw_a0, rw_a_up, rw_k_k, rw_k_a, rw_r_k, rw_gn_g, rw_gn_b, swa_q_g, swa_k_g, swa_sinks, mem_norm_g, w_mem_kv, mem_q_g, mem_k_g, w_out):
    B, S, D = x.shape
    assert B == 1
    params = (norm_g, w_in, conv_w, conv_b, lru_wa, lru_ba, lru_wx, lru_bx, lru_lambda,
              rw_mu, rw_w0, rw_w_up, rw_a0, rw_a_up, rw_k_k, rw_k_a, rw_r_k, rw_gn_g, rw_gn_b,
              swa_q_g, swa_k_g, swa_sinks, mem_norm_g, w_mem_kv, mem_q_g, mem_k_g, w_out)
    x2 = x.reshape(S, D)
    mem2 = mem.reshape(mem.shape[1], D)
    tiles = _tiles(S)
    for l in range(norm_g.shape[0]):
        x2 = _layer(x2, mem2, tuple(t[l] for t in params), tiles, passes=1)
    return x2.reshape(B, S, D)
```

```python
import functools

import jax
import jax.numpy as jnp
from jax import lax
from jax.experimental import pallas as pl
from jax.experimental.pallas import tpu as pltpu

F32 = jnp.float32
BF16 = jnp.bfloat16

LANES = 128
SUBLANES = 8
GROUP_W = 512
HEAD64 = 64
CHUNK = 64
CONV_W = 4
LRU_C = 8.0
LORA = 32
EPS = 1e-6
RWKV_GN_EPS = 64e-5
WINDOW = 128
N_MEM = 256
MEM_HEADS = 4
NEG_BIG = -1e30
VMEM_LIMIT = 56 * 1024 * 1024

COL_LRU_X, COL_LRU_G = 0, 512
COL_RW_R, COL_RW_K, COL_RW_V, COL_RW_G = 1024, 1536, 2048, 2560
COL_SWA_Q, COL_SWA_G = 3072, 3584
COL_MEM_Q, COL_MEM_G = 4096, 4608
COL_SWA_KV, COL_RW_LORA = 5120, 5376
IN_PAD = 5632

NN = (((1,), (0,)), ((), ()))
NT = (((1,), (1,)), ((), ()))
TN = (((0,), (0,)), ((), ()))
BNN = (((2,), (1,)), ((0,), (0,)))
BNT = (((2,), (2,)), ((0,), (0,)))
BTN = (((1,), (1,)), ((0,), (0,)))


def _dot(a, b, dims=NN):
    return lax.dot_general(a, b, dims, preferred_element_type=F32)


def _split(a):
    hi = a.astype(BF16)
    lo = (a - hi.astype(F32)).astype(BF16)
    return hi, lo


def _mm(a, b, dims=NN, passes=1):
    if passes == 1:
        return _dot(a.astype(BF16), b.astype(BF16), dims)
    ah, al = _split(a)
    bh, bl = _split(b)
    return _dot(ah, bh, dims) + (_dot(ah, bl, dims) + _dot(al, bh, dims))


def _mm_exact_rhs(a, b_bf16):
    ah, al = _split(a)
    return _dot(ah, b_bf16) + _dot(al, b_bf16)


def _iota2(shape, axis):
    return lax.broadcasted_iota(jnp.int32, shape, axis)


def _head_ones():
    r = _iota2((LANES, LANES), 0) // HEAD64
    c = _iota2((LANES, LANES), 1) // HEAD64
    return (r == c).astype(BF16)


def _segsum64(x, ones_bd):
    n, w = x.shape
    cols = [_mm_exact_rhs(x[:, j:j + LANES], ones_bd) for j in range(0, w, LANES)]
    return cols[0] if len(cols) == 1 else jnp.concatenate(cols, axis=1)


def _norm64(x, gain):
    ss = _segsum64(x * x, _head_ones())
    return x * lax.rsqrt(ss * (1.0 / HEAD64) + EPS) * gain


def _softplus(z):
    return jnp.maximum(z, 0.0) + jnp.log1p(jnp.exp(-jnp.abs(z)))


def _silu(z):
    return z * jax.nn.sigmoid(z)


def _inproj_kernel(x_ref, g_ref, w_ref, o_ref, h_ref):
    @pl.when(pl.program_id(1) == 0)
    def _():
        x = x_ref[...]
        ms = jnp.mean(x * x, axis=-1, keepdims=True)
        h_ref[...] = (x * lax.rsqrt(ms + EPS) * g_ref[...]).astype(BF16)

    o_ref[...] = _dot(h_ref[...], w_ref[...])


def _inproj(x2, g, w_pad, tm, tn):
    S, D = x2.shape
    N = w_pad.shape[1]
    return pl.pallas_call(
        _inproj_kernel,
        grid=(S // tm, N // tn),
        in_specs=[
            pl.BlockSpec((tm, D), lambda i, j: (i, 0)),
            pl.BlockSpec((1, D), lambda i, j: (0, 0)),
            pl.BlockSpec((D, tn), lambda i, j: (0, j)),
        ],
        out_specs=pl.BlockSpec((tm, tn), lambda i, j: (i, j)),
        out_shape=jax.ShapeDtypeStruct((S, N), F32),
        scratch_shapes=[pltpu.VMEM((tm, D), BF16)],
        compiler_params=pltpu.CompilerParams(
            dimension_semantics=("arbitrary", "arbitrary"), vmem_limit_bytes=VMEM_LIMIT),
        name="inproj",
    )(x2, g, w_pad)


def _lru_block(j, x_ref, g_ref, cw_ref, cb_ref, wa_ref, ba_ref, wx_ref, bx_ref, lam_ref,
               xbuf_ref, h_ref):
    tm = x_ref.shape[0]
    c0, c1 = j * LANES, (j + 1) * LANES
    x = x_ref[:, c0:c1]
    xbuf_ref[8:8 + tm, c0:c1] = x
    conv = cb_ref[:, c0:c1] + cw_ref[CONV_W - 1:CONV_W, c0:c1] * x
    for t in range(CONV_W - 1):
        off = 8 - (CONV_W - 1) + t
        conv = conv + cw_ref[t:t + 1, c0:c1] * xbuf_ref[off:off + tm, c0:c1]
    xbuf_ref[0:8, c0:c1] = x[tm - 8:tm, :]

    cb16 = conv.astype(BF16)
    r = jax.nn.sigmoid(_dot(cb16, wa_ref[c0:c1, c0:c1]) + ba_ref[:, c0:c1])
    ig = jax.nn.sigmoid(_dot(cb16, wx_ref[c0:c1, c0:c1]) + bx_ref[:, c0:c1])
    log_a = (-LRU_C) * r * _softplus(-lam_ref[:, c0:c1])
    a = jnp.exp(log_a)
    mult = jnp.sqrt(jnp.maximum(-jnp.tanh(log_a) * (a * a + 1.0), 1e-12))
    u = mult * (ig * conv)

    n_slabs = tm // SUBLANES
    a3 = a.reshape(n_slabs, SUBLANES, LANES)
    u3 = u.reshape(n_slabs, SUBLANES, LANES)
    sub = lax.broadcasted_iota(jnp.int32, (1, SUBLANES, LANES), 1)
    d = 1
    while d < SUBLANES:
        keep = sub >= d
        a_sh = jnp.where(keep, pltpu.roll(a3, d, 1), 1.0)
        u_sh = jnp.where(keep, pltpu.roll(u3, d, 1), 0.0)
        u3 = u3 + a3 * u_sh
        a3 = a3 * a_sh
        d *= 2
    h = h_ref[:, c0:c1]
    slabs = []
    for s in range(n_slabs):
        h_s = u3[s] + a3[s] * h
        slabs.append(h_s)
        h = h_s[SUBLANES - 1:SUBLANES, :]
    h_ref[:, c0:c1] = h
    return jnp.concatenate(slabs, axis=0) * _silu(g_ref[:, c0:c1])


def _rwkv_kernel(r_ref, k_ref, v_ref, l_ref, g_ref, mu_ref, w0_ref, wup_ref, a0_ref, aup_ref,
                 kk_ref, ka_ref, rk_ref, gng_ref, gnb_ref, o_ref,
                 prev_ref, state_ref, *, passes):
    tm = r_ref.shape[0]
    n_chunks = tm // CHUNK
    n_pairs = GROUP_W // LANES
    WIN = 3 * GROUP_W + LANES

    @pl.when(pl.program_id(0) == 0)
    def _():
        prev_ref[...] = jnp.zeros_like(prev_ref)
        state_ref[...] = jnp.zeros_like(state_ref)

    row_t = _iota2((tm, 1), 0)

    def shift_lerp(ref, c0, c1):
        x = ref[...]
        xs = jnp.where(row_t == 0, prev_ref[:, c0:c1], pltpu.roll(x, 1, 0))
        prev_ref[:, c0:c1] = x[tm - 1:tm, :]
        return x + (xs - x) * mu_ref[:, c0:c1]

    r = shift_lerp(r_ref, 0, GROUP_W)
    k = shift_lerp(k_ref, GROUP_W, 2 * GROUP_W)
    v = shift_lerp(v_ref, 2 * GROUP_W, 3 * GROUP_W)
    lo_in = shift_lerp(l_ref, 3 * GROUP_W, WIN)

    lane_l = _iota2((tm, LANES), 1)
    lora = jnp.where(lane_l < LORA, jnp.tanh(lo_in), lo_in)
    w = -_softplus(-(w0_ref[...] + _mm(lora, wup_ref[...], passes=3))) - 0.5
    lw = -jnp.exp(w)
    a = jax.nn.sigmoid(a0_ref[...] + _mm(lora, aup_ref[...], passes=3))

    ones_bd = _head_ones()
    kk = k * kk_ref[...]
    kk = kk / jnp.maximum(jnp.sqrt(_segsum64(kk * kk, ones_bd)), 1e-12)
    k2 = k * (1.0 + (a - 1.0) * ka_ref[...])
    b = kk * a
    bonus = _segsum64(r * k2 * rk_ref[...], ones_bd) * v

    cl = lw
    rows_c = _iota2((tm, GROUP_W), 0) % CHUNK
    d = 1
    while d < CHUNK:
        cl = cl + jnp.where(rows_c >= d, pltpu.roll(cl, d, 0), 0.0)
        d *= 2

    to3 = lambda t: t.reshape(n_chunks, CHUNK, GROUP_W)
    cl3 = to3(cl)
    cl_end = cl3[:, CHUNK - 1:CHUNK, :]
    e_inv = jnp.exp(-cl3)
    e_end = jnp.exp(cl_end - cl3)
    w_end = jnp.exp(cl_end)
    kk3, b3, k23 = to3(kk), to3(b), to3(k2)

    lo_lane = lax.broadcasted_iota(jnp.int32, (1, CHUNK, LANES), 2) < HEAD64
    n_units = n_chunks * n_pairs

    def stack(t3, dtype):
        units = []
        for p in range(n_pairs):
            x = t3[:, :, p * LANES:(p + 1) * LANES]
            units.append(jnp.concatenate(
                [jnp.where(lo_lane, x, 0.0), jnp.where(lo_lane, 0.0, x)], axis=1).astype(dtype))
        return jnp.stack(units, axis=1).reshape(n_units, 2 * CHUNK, LANES)

    op_dtype = BF16 if passes == 1 else F32
    cast = lambda t: t.astype(op_dtype)
    mm = functools.partial(_mm, passes=passes)

    a_s = stack(-kk3 * jnp.exp(cl3 - to3(lw)), op_dtype)
    r_s = stack(to3(r) * jnp.exp(cl3), F32)
    b_s = stack(b3 * e_inv, op_dtype)
    k_s = stack(k23 * e_inv, op_dtype)
    be_s = stack(b3 * e_end, op_dtype)
    ke_s = stack(k23 * e_end, op_dtype)
    v_s = stack(to3(v), op_dtype)
    wend_u = jnp.stack([w_end[:, :, p * LANES:(p + 1) * LANES] for p in range(n_pairs)],
                       axis=1).reshape(n_units, 1, LANES)

    ri = lax.broadcasted_iota(jnp.int32, (1, LANES, LANES), 1)
    ci = lax.broadcasted_iota(jnp.int32, (1, LANES, LANES), 2)
    same_head = (ri // CHUNK) == (ci // CHUNK)
    strict = same_head & (ci < ri)
    incl = same_head & (ci <= ri)
    diag16 = strict & ((ri // 16) == (ci // 16))
    off32 = strict & ((ri // 32) == (ci // 32)) & ((ri // 16) != (ci // 16))
    off64 = strict & ((ri // 32) != (ci // 32))
    eye = (ri == ci).astype(F32)

    g = mm(jnp.concatenate([a_s, cast(r_s)], axis=1), jnp.concatenate([b_s, k_s], axis=1), BNT)
    g_ab, g_ak = g[:, 0:LANES, 0:LANES], g[:, 0:LANES, LANES:2 * LANES]
    g_rb, g_rk = g[:, LANES:2 * LANES, 0:LANES], g[:, LANES:2 * LANES, LANES:2 * LANES]
    l_ak = cast(jnp.where(strict, g_ak, 0.0))
    m_rb = cast(jnp.where(incl, g_rb, 0.0))
    m_rk = cast(jnp.where(incl, g_rk, 0.0))

    d1 = jnp.where(diag16, g_ab, 0.0)
    d1c = cast(d1)
    d2 = cast(mm(d1c, d1c, BNN))
    d4 = cast(mm(d2, d2, BNN))
    d8 = mm(d4, d4, BNN)
    t = eye + d1
    t = t + mm(cast(t), d2, BNN)
    t = t + mm(cast(t), d4, BNN)
    t = t + mm(cast(t), d8, BNN)
    tc = cast(t)
    t = t + mm(tc, mm(cast(jnp.where(off32, g_ab, 0.0)), tc, BNN), BNN)
    tc = cast(t)
    t = t + mm(tc, mm(cast(jnp.where(off64, g_ab, 0.0)), tc, BNN), BNN)

    lv = mm(l_ak, v_s, BNN)
    tu = cast(mm(cast(t), jnp.concatenate([a_s, cast(lv)], axis=2), BNN))
    ru = mm(m_rb, tu, BNN)
    r_hat = cast(r_s + ru[:, :, 0:LANES])
    y_v = ru[:, :, LANES:2 * LANES] + mm(m_rk, v_s, BNN)
    pq = mm(be_s, tu, BTN)
    pp = cast(pq[:, :, 0:LANES])
    q = pq[:, :, LANES:2 * LANES] + mm(ke_s, v_s, BTN)
    wcol = jnp.sum(eye * wend_u, axis=2, keepdims=True)

    s0 = state_ref[...]
    y_parts = []
    for c in range(n_chunks):
        u0, u1 = c * n_pairs, (c + 1) * n_pairs
        s0c = cast(s0)
        y_parts.append(mm(r_hat[u0:u1], s0c, BNN) + y_v[u0:u1])
        s0 = wcol[u0:u1] * s0 + mm(pp[u0:u1], s0c, BNN) + q[u0:u1]
    state_ref[...] = s0
    y_s = jnp.concatenate(y_parts, axis=0)
    y4 = (y_s[:, 0:CHUNK, :] + y_s[:, CHUNK:2 * CHUNK, :]).reshape(n_chunks, n_pairs, CHUNK, LANES)
    y = jnp.concatenate([y4[:, p] for p in range(n_pairs)], axis=-1).reshape(tm, GROUP_W)
    inv_n = 1.0 / HEAD64
    mean = _segsum64(y, ones_bd) * inv_n
    yc = y - mean
    var = _segsum64(yc * yc, ones_bd) * inv_n
    yn = yc * lax.rsqrt(var + RWKV_GN_EPS) * gng_ref[...] + gnb_ref[...]
    o_ref[...] = (yn + bonus) * _silu(g_ref[...])


def _rwkv(p, mu, w0, wup, a0, aup, kkw, kaw, rkw, gng, gnb, tm, passes):
    S = p.shape[0]
    row = lambda n=1, w=GROUP_W: pl.BlockSpec((n, w), lambda i: (0, 0))
    col = lambda off, w=GROUP_W: pl.BlockSpec((tm, w), lambda i: (i, off // w))
    WIN = 3 * GROUP_W + LANES
    return pl.pallas_call(
        functools.partial(_rwkv_kernel, passes=passes),
        grid=(S // tm,),
        in_specs=[
            col(COL_RW_R), col(COL_RW_K), col(COL_RW_V), col(COL_RW_LORA, LANES), col(COL_RW_G),
            row(1, WIN), row(), row(LANES), row(), row(LANES),
            row(), row(), row(), row(), row(),
        ],
        out_specs=pl.BlockSpec((tm, GROUP_W), lambda i: (i, 0)),
        out_shape=jax.ShapeDtypeStruct((S, GROUP_W), F32),
        scratch_shapes=[
            pltpu.VMEM((1, WIN), F32),
            pltpu.VMEM((GROUP_W // LANES, LANES, LANES), F32),
        ],
        compiler_params=pltpu.CompilerParams(
            dimension_semantics=("arbitrary",), vmem_limit_bytes=VMEM_LIMIT),
        name="rwkv7",
    )(p, p, p, p, p, mu, w0, wup, a0, aup, kkw, kaw, rkw, gng, gnb)


def _swa_prepare(kv_ref, kvp_ref, kg_ref):
    tq = kv_ref.shape[0]
    nblk = tq // WINDOW
    kv_all = jnp.concatenate([kvp_ref[...], kv_ref[...]], axis=0)
    k_all = _norm64(kv_all[:, 0:LANES], kg_ref[...]).astype(BF16)
    v_all = kv_all[:, LANES:2 * LANES]
    lo_v = _iota2((WINDOW + tq, LANES), 1) < HEAD64
    v_lo_all = jnp.where(lo_v, v_all, 0.0).astype(BF16)
    v_hi_all = jnp.where(lo_v, 0.0, v_all).astype(BF16)

    window = lambda t: jnp.stack(
        [t[b * WINDOW:(b + 2) * WINDOW, :] for b in range(nblk)], axis=0)

    shape3 = (nblk, 2 * WINDOW, 2 * WINDOW)
    qi = lax.broadcasted_iota(jnp.int32, shape3, 1) % WINDOW
    kj = lax.broadcasted_iota(jnp.int32, shape3, 2)
    blk = lax.broadcasted_iota(jnp.int32, shape3, 0)
    has_prev = (kj >= WINDOW) | (blk > 0) | (pl.program_id(0) > 0)
    mask = (kj > qi) & (kj <= qi + WINDOW) & has_prev
    return window(k_all), window(v_lo_all), window(v_hi_all), mask


def _swa_block(i, prepared, sink_ref, q_ref, g_ref, qg_ref):
    keys, v_lo, v_hi, mask = prepared
    tq = q_ref.shape[0]
    nblk = tq // WINDOW
    c0, c1 = i * LANES, (i + 1) * LANES
    upper_rows = lax.broadcasted_iota(jnp.int32, (1, 2 * WINDOW, 1), 1) >= WINDOW
    lo_q = lax.broadcasted_iota(jnp.int32, (1, WINDOW, LANES), 2) < HEAD64
    qn = _norm64(q_ref[:, c0:c1], qg_ref[...] * (HEAD64 ** -0.5)).reshape(nblk, WINDOW, LANES)
    q_s = jnp.concatenate([jnp.where(lo_q, qn, 0.0), jnp.where(lo_q, 0.0, qn)],
                          axis=1).astype(BF16)
    s = jnp.where(mask, _dot(q_s, keys, BNT), NEG_BIG)
    sink = jnp.where(upper_rows, sink_ref[4 + i], sink_ref[i])
    m = jnp.maximum(jnp.max(s, axis=-1, keepdims=True), sink)
    e = jnp.exp(s - m)
    rden = 1.0 / (jnp.sum(e, axis=-1, keepdims=True) + jnp.exp(sink - m))
    e = e.astype(BF16)
    out = (_dot(e[:, 0:WINDOW, :], v_lo, BNN) * rden[:, 0:WINDOW, :]
           + _dot(e[:, WINDOW:2 * WINDOW, :], v_hi, BNN) * rden[:, WINDOW:2 * WINDOW, :])
    return out.reshape(tq, LANES) * _silu(g_ref[:, c0:c1])


def _memkv_kernel(m_ref, g_ref, w_ref, kg_ref, o_ref, h_ref):
    j = pl.program_id(0)

    @pl.when(j == 0)
    def _():
        x = m_ref[...]
        ms = jnp.mean(x * x, axis=-1, keepdims=True)
        h_ref[...] = (x * lax.rsqrt(ms + EPS) * g_ref[...]).astype(BF16)

    kv = _dot(h_ref[...], w_ref[...])

    @pl.when(j < MEM_HEADS)
    def _():
        ms = jnp.mean(kv * kv, axis=-1, keepdims=True)
        o_ref[...] = kv * lax.rsqrt(ms + EPS) * kg_ref[...]

    @pl.when(j >= MEM_HEADS)
    def _():
        o_ref[...] = kv


def _memkv(mem2, g, w_kv, kg):
    M, D = mem2.shape
    N = w_kv.shape[1]
    return pl.pallas_call(
        _memkv_kernel,
        grid=(N // LANES,),
        in_specs=[
            pl.BlockSpec((M, D), lambda j: (0, 0)),
            pl.BlockSpec((1, D), lambda j: (0, 0)),
            pl.BlockSpec((D, LANES), lambda j: (0, j)),
            pl.BlockSpec((1, LANES), lambda j: (0, 0)),
        ],
        out_specs=pl.BlockSpec((M, LANES), lambda j: (0, j)),
        out_shape=jax.ShapeDtypeStruct((M, N), F32),
        scratch_shapes=[pltpu.VMEM((M, D), BF16)],
        compiler_params=pltpu.CompilerParams(
            dimension_semantics=("arbitrary",), vmem_limit_bytes=VMEM_LIMIT),
        name="memkv",
    )(mem2, g, w_kv, kg)


def _mem_head(h, q_ref, g_ref, kv_ref, qg_ref):
    head = GROUP_W // MEM_HEADS
    c0, c1 = h * head, (h + 1) * head
    q = q_ref[:, c0:c1]
    gain = qg_ref[...] * (head ** -0.5)
    qn = q * lax.rsqrt(jnp.mean(q * q, axis=-1, keepdims=True) + EPS) * gain
    s = _mm(qn, kv_ref[:, c0:c1], NT)
    m = jnp.max(s, axis=-1, keepdims=True)
    e = jnp.exp(s - m)
    rden = 1.0 / jnp.sum(e, axis=-1, keepdims=True)
    out = _mm(e, kv_ref[:, GROUP_W + c0:GROUP_W + c1]) * rden
    return out * _silu(g_ref[:, c0:c1])


def _post_kernel(sink_ref, x_ref, orw_ref, lx_ref, lg_ref, sq_ref, skv_ref, skvp_ref, sg_ref,
                 mq_ref, mg_ref, cw_ref, cb_ref, wa_ref, ba_ref, wx_ref, bx_ref, lam_ref,
                 sqg_ref, skg_ref, mkv_ref, mqg_ref, w_ref, y_ref, xbuf_ref, h_ref):
    @pl.when(pl.program_id(0) == 0)
    def _():
        xbuf_ref[0:8, :] = jnp.zeros((8, GROUP_W), F32)
        h_ref[...] = jnp.zeros_like(h_ref)

    n_stage = GROUP_W // LANES
    piece = y_ref.shape[1] // n_stage

    def project(o16, group, j):
        return _dot(o16, w_ref[group * GROUP_W:(group + 1) * GROUP_W, j * piece:(j + 1) * piece])

    def accumulate(stage_fn, o16, group, first=False):
        cols = []
        for j in range(n_stage):
            cols.append(stage_fn(j))
            n0, n1 = j * piece, (j + 1) * piece
            base = x_ref[:, n0:n1] if first else y_ref[:, n0:n1]
            y_ref[:, n0:n1] = base + project(o16, group, j)
        return jnp.concatenate(cols, axis=1).astype(BF16)

    lru = functools.partial(_lru_block, x_ref=lx_ref, g_ref=lg_ref, cw_ref=cw_ref, cb_ref=cb_ref,
                            wa_ref=wa_ref, ba_ref=ba_ref, wx_ref=wx_ref, bx_ref=bx_ref,
                            lam_ref=lam_ref, xbuf_ref=xbuf_ref, h_ref=h_ref)
    o_lru = accumulate(lru, orw_ref[...].astype(BF16), 1, first=True)
    prepared = _swa_prepare(skv_ref, skvp_ref, skg_ref)
    swa = functools.partial(_swa_block, prepared=prepared, sink_ref=sink_ref, q_ref=sq_ref,
                            g_ref=sg_ref, qg_ref=sqg_ref)
    o_swa = accumulate(swa, o_lru, 0)
    mem = functools.partial(_mem_head, q_ref=mq_ref, g_ref=mg_ref, kv_ref=mkv_ref, qg_ref=mqg_ref)
    o_mem = accumulate(mem, o_swa, 2)
    for j in range(n_stage):
        n0, n1 = j * piece, (j + 1) * piece
        y_ref[:, n0:n1] = y_ref[:, n0:n1] + project(o_mem, 3, j)


def _post(x2, p, o_rw, sinks, lru_prm, swa_prm, mem_prm, w_out, tm):
    S, D = x2.shape
    nblk = tm // WINDOW
    col = lambda off, w=GROUP_W: pl.BlockSpec((tm, w), lambda i: (i, off // w))
    const = lambda shape: pl.BlockSpec(shape, lambda i: (0,) * len(shape))
    cw, cb, wa_bd, ba, wx_bd, bx, lam = lru_prm
    qg2, kg2 = swa_prm
    kv, mqg = mem_prm
    return pl.pallas_call(
        _post_kernel,
        grid=(S // tm,),
        in_specs=[
            pl.BlockSpec(memory_space=pltpu.SMEM),
            pl.BlockSpec((tm, D), lambda i: (i, 0)),
            pl.BlockSpec((tm, GROUP_W), lambda i: (i, 0)),
            col(COL_LRU_X), col(COL_LRU_G),
            col(COL_SWA_Q), col(COL_SWA_KV, 2 * LANES),
            pl.BlockSpec((WINDOW, 2 * LANES),
                         lambda i: (jnp.maximum(i * nblk - 1, 0), COL_SWA_KV // (2 * LANES))),
            col(COL_SWA_G), col(COL_MEM_Q), col(COL_MEM_G),
            const((CONV_W, GROUP_W)), const((1, GROUP_W)),
            const((GROUP_W, GROUP_W)), const((1, GROUP_W)),
            const((GROUP_W, GROUP_W)), const((1, GROUP_W)), const((1, GROUP_W)),
            const((1, LANES)), const((1, LANES)),
            const((N_MEM, 2 * GROUP_W)), const((1, GROUP_W // MEM_HEADS)),
            pl.BlockSpec((4 * GROUP_W, D), lambda i: (0, 0), pipeline_mode=pl.Buffered(1)),
        ],
        out_specs=pl.BlockSpec((tm, D), lambda i: (i, 0)),
        out_shape=jax.ShapeDtypeStruct((S, D), F32),
        scratch_shapes=[pltpu.VMEM((tm + 8, GROUP_W), F32), pltpu.VMEM((1, GROUP_W), F32)],
        compiler_params=pltpu.CompilerParams(
            dimension_semantics=("arbitrary",), vmem_limit_bytes=VMEM_LIMIT),
        name="post",
    )(sinks, x2, o_rw, p, p, p, p, p, p, p, p, cw, cb, wa_bd, ba, wx_bd, bx, lam,
      qg2, kg2, kv, mqg, w_out)


SWA_PERM = (0, 4, 1, 5, 2, 6, 3, 7)


def _perm_heads(t, axis):
    parts = jnp.split(t, 8, axis=axis)
    return jnp.concatenate([parts[h] for h in SWA_PERM], axis=axis)


def _pad_in_weights(w):
    D = w.shape[0]
    w = w.astype(BF16)
    sizes = (512, 512, 512, 512, 512, LORA, LORA, 512, 512, 128, 128, 512, 512, 512)
    offs = [0]
    for s in sizes:
        offs.append(offs[-1] + s)
    (lru_x, lru_g, rw_r, rw_k, rw_v, rw_wd, rw_ad, rw_g,
     sq, sk, sv, sg, mq, mg) = [w[:, offs[n]:offs[n + 1]] for n in range(len(sizes))]
    z = lambda n: jnp.zeros((D, n), w.dtype)
    cols = [lru_x, lru_g, rw_r, rw_k, rw_v, rw_g, _perm_heads(sq, 1), _perm_heads(sg, 1), mq, mg,
            sk, sv, rw_wd, rw_ad, z(LANES - 2 * LORA), z(LANES)]
    out = jnp.concatenate(cols, axis=1)
    assert out.shape[1] == IN_PAD
    return out


def _block_diag(w):
    nb, bw, _ = w.shape
    eye = jnp.eye(nb, dtype=w.dtype)
    return (eye[:, None, :, None] * w[:, :, None, :]).reshape(nb * bw, nb * bw)


def _layer(x2, mem2, prm, tiles, passes):
    (norm_g, w_in, conv_w, conv_b, lru_wa, lru_ba, lru_wx, lru_bx, lru_lambda,
     rw_mu, rw_w0, rw_w_up, rw_a0, rw_a_up, rw_k_k, rw_k_a, rw_r_k, rw_gn_g, rw_gn_b,
     swa_q_g, swa_k_g, swa_sinks, mem_norm_g, w_mem_kv, mem_q_g, mem_k_g, w_out) = prm
    row = lambda t: t.reshape(1, -1)

    p = _inproj(x2, row(norm_g), _pad_in_weights(w_in), tiles["inproj_m"], tiles["inproj_n"])

    mu_pad = jnp.concatenate([rw_mu, jnp.zeros((LANES - 2 * LORA,), F32)]).reshape(1, -1)
    zpad = jnp.zeros((LANES - LORA, GROUP_W), F32)
    wup_pad = jnp.concatenate([rw_w_up, zpad], axis=0)
    aup_pad = jnp.concatenate([jnp.zeros((LORA, GROUP_W), F32), rw_a_up,
                               jnp.zeros((LANES - 2 * LORA, GROUP_W), F32)], axis=0)
    o_rw = _rwkv(p, mu_pad, row(rw_w0), wup_pad, row(rw_a0), aup_pad, row(rw_k_k), row(rw_k_a),
                 row(rw_r_k), row(rw_gn_g), row(rw_gn_b), tiles["rwkv"], passes)

    kv = _memkv(mem2, row(mem_norm_g), w_mem_kv.astype(BF16), row(mem_k_g))
    two = lambda t: jnp.concatenate([t, t]).reshape(1, LANES)
    w_out_p = jnp.concatenate(
        [w_out[0:2 * GROUP_W], _perm_heads(w_out[2 * GROUP_W:3 * GROUP_W], 0), w_out[3 * GROUP_W:]],
        axis=0).astype(BF16)
    lru_prm = (conv_w, row(conv_b), _block_diag(lru_wa).astype(BF16), row(lru_ba),
               _block_diag(lru_wx).astype(BF16), row(lru_bx), row(lru_lambda))
    return _post(x2, p, o_rw, swa_sinks, lru_prm, (two(swa_q_g), two(swa_k_g)),
                 (kv, row(mem_q_g)), w_out_p, tiles["post"])


def _tiles(S):
    pick = lambda t: t if S % t == 0 else S
    return {"inproj_m": pick(512), "inproj_n": IN_PAD // 2, "rwkv": pick(256), "post": pick(512)}


def kernel(x, mem, norm_g, w_in, conv_w, conv_b, lru_wa, lru_ba, lru_wx, lru_bx, lru_lambda, rw_mu, rw_w0, rw_w_up, rw_a0, rw_a_up, rw_k_k, rw_k_a, rw_r_k, rw_gn_g, rw_gn_b, swa_q_g, swa_k_g, swa_sinks, mem_norm_g, w_mem_kv, mem_q_g, mem_k_g, w_out):
    B, S, D = x.shape
    assert B == 1
    params = (norm_g, w_in, conv_w, conv_b, lru_wa, lru_ba, lru_wx, lru_bx, lru_lambda,
              rw_mu, rw_w0, rw_w_up, rw_a0, rw_a_up, rw_k_k, rw_k_a, rw_r_k, rw_gn_g, rw_gn_b,
              swa_q_g, swa_k_g, swa_sinks, mem_norm_g, w_mem_kv, mem_q_g, mem_k_g, w_out)
    x2 = x.reshape(S, D)
    mem2 = mem.reshape(mem.shape[1], D)
    tiles = _tiles(S)
    for l in range(norm_g.shape[0]):
        x2 = _layer(x2, mem2, tuple(t[l] for t in params), tiles, passes=1)
    return x2.reshape(B, S, D)
```

```python
import functools

import jax
import jax.numpy as jnp
from jax import lax
from jax.experimental import pallas as pl
from jax.experimental.pallas import tpu as pltpu

F32 = jnp.float32
BF16 = jnp.bfloat16

LANES = 128
SUBLANES = 8
GROUP_W = 512
HEAD64 = 64
CHUNK = 64
CONV_W = 4
LRU_C = 8.0
LORA = 32
EPS = 1e-6
RWKV_GN_EPS = 64e-5
WINDOW = 128
N_MEM = 256
MEM_HEADS = 4
NEG_BIG = -1e30
VMEM_LIMIT = 56 * 1024 * 1024

COL_LRU_X, COL_LRU_G = 0, 512
COL_RW_R, COL_RW_K, COL_RW_V, COL_RW_G = 1024, 1536, 2048, 2560
COL_SWA_Q, COL_SWA_G = 3072, 3584
COL_MEM_Q, COL_MEM_G = 4096, 4608
COL_SWA_KV, COL_RW_LORA = 5120, 5376
IN_PAD = 5632
IN_HEAD = 2560

NN = (((1,), (0,)), ((), ()))
NT = (((1,), (1,)), ((), ()))
TN = (((0,), (0,)), ((), ()))
BNN = (((2,), (1,)), ((0,), (0,)))
BNT = (((2,), (2,)), ((0,), (0,)))
BTN = (((1,), (1,)), ((0,), (0,)))


def _dot(a, b, dims=NN):
    return lax.dot_general(a, b, dims, preferred_element_type=F32)


def _split(a):
    hi = a.astype(BF16)
    lo = (a - hi.astype(F32)).astype(BF16)
    return hi, lo


def _mm(a, b, dims=NN, passes=1):
    if passes == 1:
        return _dot(a.astype(BF16), b.astype(BF16), dims)
    ah, al = _split(a)
    bh, bl = _split(b)
    return _dot(ah, bh, dims) + (_dot(ah, bl, dims) + _dot(al, bh, dims))


def _mm_exact_rhs(a, b_bf16):
    ah, al = _split(a)
    return _dot(ah, b_bf16) + _dot(al, b_bf16)


def _iota2(shape, axis):
    return lax.broadcasted_iota(jnp.int32, shape, axis)


def _head_ones():
    r = _iota2((LANES, LANES), 0) // HEAD64
    c = _iota2((LANES, LANES), 1) // HEAD64
    return (r == c).astype(BF16)


def _segsum64(x, ones_bd):
    n, w = x.shape
    cols = [_mm_exact_rhs(x[:, j:j + LANES], ones_bd) for j in range(0, w, LANES)]
    return cols[0] if len(cols) == 1 else jnp.concatenate(cols, axis=1)


def _norm64(x, gain):
    ss = _segsum64(x * x, _head_ones())
    return x * lax.rsqrt(ss * (1.0 / HEAD64) + EPS) * gain


def _softplus(z):
    return jnp.maximum(z, 0.0) + jnp.log1p(jnp.exp(-jnp.abs(z)))


def _sigmoid(z):
    return 0.5 * jnp.tanh(0.5 * z) + 0.5


def _silu(z):
    return z * _sigmoid(z)


def _inproj_kernel(x_ref, g_ref, wh_ref, wt_ref, o_ref):
    x = x_ref[...]
    ms = jnp.mean(x * x, axis=-1, keepdims=True)
    h = (x * lax.rsqrt(ms + EPS) * g_ref[...]).astype(BF16)
    o_ref[:, 0:IN_HEAD] = _dot(h, wh_ref[...])
    o_ref[:, IN_HEAD:IN_PAD] = _dot(h, wt_ref[...])


def _inproj(x2, g, w_all, layer, w_tail, tm):
    S, D = x2.shape
    once = pl.Buffered(1)
    return pl.pallas_call(
        _inproj_kernel,
        grid=(S // tm,),
        in_specs=[
            pl.BlockSpec((tm, D), lambda i: (i, 0)),
            pl.BlockSpec((1, D), lambda i: (0, 0)),
            pl.BlockSpec((None, D, IN_HEAD), lambda i: (layer, 0, 0), pipeline_mode=once),
            pl.BlockSpec((D, IN_PAD - IN_HEAD), lambda i: (0, 0), pipeline_mode=once),
        ],
        out_specs=pl.BlockSpec((tm, IN_PAD), lambda i: (i, 0)),
        out_shape=jax.ShapeDtypeStruct((S, IN_PAD), F32),
        compiler_params=pltpu.CompilerParams(
            dimension_semantics=("arbitrary",), vmem_limit_bytes=VMEM_LIMIT),
        name="inproj",
    )(x2, g, w_all, w_tail)


def _lru_block(j, x_ref, g_ref, cw_ref, cb_ref, wa_ref, ba_ref, wx_ref, bx_ref, lam_ref,
               xbuf_ref, h_ref):
    tm = x_ref.shape[0]
    c0, c1 = j * LANES, (j + 1) * LANES
    x = x_ref[:, c0:c1]
    xbuf_ref[8:8 + tm, c0:c1] = x
    conv = cb_ref[:, c0:c1] + cw_ref[CONV_W - 1:CONV_W, c0:c1] * x
    for t in range(CONV_W - 1):
        off = 8 - (CONV_W - 1) + t
        conv = conv + cw_ref[t:t + 1, c0:c1] * xbuf_ref[off:off + tm, c0:c1]
    xbuf_ref[0:8, c0:c1] = x[tm - 8:tm, :]

    cb16 = conv.astype(BF16)
    r = _sigmoid(_dot(cb16, wa_ref[c0:c1, c0:c1]) + ba_ref[:, c0:c1])
    ig = _sigmoid(_dot(cb16, wx_ref[c0:c1, c0:c1]) + bx_ref[:, c0:c1])
    log_a = (-LRU_C) * r * _softplus(-lam_ref[:, c0:c1])
    a = jnp.exp(log_a)
    m2 = jnp.maximum(-jnp.tanh(log_a) * (a * a + 1.0), 1e-12)
    u = (m2 * lax.rsqrt(m2)) * (ig * conv)

    n_slabs = tm // SUBLANES
    a3 = a.reshape(n_slabs, SUBLANES, LANES)
    u3 = u.reshape(n_slabs, SUBLANES, LANES)
    sub = lax.broadcasted_iota(jnp.int32, (1, SUBLANES, LANES), 1)
    d = 1
    while d < SUBLANES:
        keep = sub >= d
        a_sh = jnp.where(keep, pltpu.roll(a3, d, 1), 1.0)
        u_sh = jnp.where(keep, pltpu.roll(u3, d, 1), 0.0)
        u3 = u3 + a3 * u_sh
        a3 = a3 * a_sh
        d *= 2
    h = h_ref[:, c0:c1]
    slabs = []
    for s in range(n_slabs):
        h_s = u3[s] + a3[s] * h
        slabs.append(h_s)
        h = h_s[SUBLANES - 1:SUBLANES, :]
    h_ref[:, c0:c1] = h
    return jnp.concatenate(slabs, axis=0) * _silu(g_ref[:, c0:c1])


def _rwkv_kernel(r_ref, k_ref, v_ref, l_ref, g_ref, mu_ref, w0_ref, wup_ref, a0_ref, aup_ref,
                 kk_ref, ka_ref, rk_ref, gng_ref, gnb_ref, o_ref,
                 prev_ref, state_ref, *, passes):
    tm = r_ref.shape[0]
    n_chunks = tm // CHUNK
    n_pairs = GROUP_W // LANES
    WIN = 3 * GROUP_W + LANES

    @pl.when(pl.program_id(0) == 0)
    def _():
        prev_ref[...] = jnp.zeros_like(prev_ref)
        state_ref[...] = jnp.zeros_like(state_ref)

    row_t = _iota2((tm, 1), 0)

    def shift_lerp(ref, c0, c1):
        x = ref[...]
        xs = jnp.where(row_t == 0, prev_ref[:, c0:c1], pltpu.roll(x, 1, 0))
        prev_ref[:, c0:c1] = x[tm - 1:tm, :]
        return x + (xs - x) * mu_ref[:, c0:c1]

    r = shift_lerp(r_ref, 0, GROUP_W)
    k = shift_lerp(k_ref, GROUP_W, 2 * GROUP_W)
    v = shift_lerp(v_ref, 2 * GROUP_W, 3 * GROUP_W)
    lo_in = shift_lerp(l_ref, 3 * GROUP_W, WIN)

    lane_l = _iota2((tm, LANES), 1)
    lora = jnp.where(lane_l < LORA, jnp.tanh(lo_in), lo_in)
    w = -_softplus(-(w0_ref[...] + _mm(lora, wup_ref[...], passes=3))) - 0.5
    lw = -jnp.exp(w)
    a = _sigmoid(a0_ref[...] + _mm(lora, aup_ref[...], passes=3))

    ones_bd = _head_ones()
    kk = k * kk_ref[...]
    kk = kk / jnp.maximum(jnp.sqrt(_segsum64(kk * kk, ones_bd)), 1e-12)
    k2 = k * (1.0 + (a - 1.0) * ka_ref[...])
    b = kk * a
    bonus = _segsum64(r * k2 * rk_ref[...], ones_bd) * v

    cl = lw
    rows_c = _iota2((tm, GROUP_W), 0) % CHUNK
    d = 1
    while d < CHUNK:
        cl = cl + jnp.where(rows_c >= d, pltpu.roll(cl, d, 0), 0.0)
        d *= 2

    to3 = lambda t: t.reshape(n_chunks, CHUNK, GROUP_W)
    cl3 = to3(cl)
    cl_end = cl3[:, CHUNK - 1:CHUNK, :]
    e_inv = jnp.exp(-cl3)
    e_end = jnp.exp(cl_end - cl3)
    w_end = jnp.exp(cl_end)
    kk3, b3, k23 = to3(kk), to3(b), to3(k2)

    lo_lane = lax.broadcasted_iota(jnp.int32, (1, CHUNK, LANES), 2) < HEAD64
    n_units = n_chunks * n_pairs

    def stack(t3, dtype):
        units = []
        for p in range(n_pairs):
            x = t3[:, :, p * LANES:(p + 1) * LANES]
            units.append(jnp.concatenate(
                [jnp.where(lo_lane, x, 0.0), jnp.where(lo_lane, 0.0, x)], axis=1).astype(dtype))
        return jnp.stack(units, axis=1).reshape(n_units, 2 * CHUNK, LANES)

    op_dtype = BF16 if passes == 1 else F32
    cast = lambda t: t.astype(op_dtype)
    mm = functools.partial(_mm, passes=passes)

    a_s = stack(-kk3 * jnp.exp(cl3 - to3(lw)), op_dtype)
    r_s = stack(to3(r) * jnp.exp(cl3), F32)
    b_s = stack(b3 * e_inv, op_dtype)
    k_s = stack(k23 * e_inv, op_dtype)
    be_s = stack(b3 * e_end, op_dtype)
    ke_s = stack(k23 * e_end, op_dtype)
    v_s = stack(to3(v), op_dtype)
    wend_u = jnp.stack([w_end[:, :, p * LANES:(p + 1) * LANES] for p in range(n_pairs)],
                       axis=1).reshape(n_units, 1, LANES)

    ri = lax.broadcasted_iota(jnp.int32, (1, LANES, LANES), 1)
    ci = lax.broadcasted_iota(jnp.int32, (1, LANES, LANES), 2)
    same_head = (ri // CHUNK) == (ci // CHUNK)
    strict = same_head & (ci < ri)
    incl = same_head & (ci <= ri)
    diag16 = strict & ((ri // 16) == (ci // 16))
    off32 = strict & ((ri // 32) == (ci // 32)) & ((ri // 16) != (ci // 16))
    off64 = strict & ((ri // 32) != (ci // 32))
    eye = (ri == ci).astype(F32)

    g = mm(jnp.concatenate([a_s, cast(r_s)], axis=1), jnp.concatenate([b_s, k_s], axis=1), BNT)
    g_ab, g_ak = g[:, 0:LANES, 0:LANES], g[:, 0:LANES, LANES:2 * LANES]
    g_rb, g_rk = g[:, LANES:2 * LANES, 0:LANES], g[:, LANES:2 * LANES, LANES:2 * LANES]
    l_ak = cast(jnp.where(strict, g_ak, 0.0))
    m_rb = cast(jnp.where(incl, g_rb, 0.0))
    m_rk = cast(jnp.where(incl, g_rk, 0.0))

    d1 = jnp.where(diag16, g_ab, 0.0)
    d1c = cast(d1)
    d2 = cast(mm(d1c, d1c, BNN))
    d4 = cast(mm(d2, d2, BNN))
    d8 = mm(d4, d4, BNN)
    t = eye + d1
    t = t + mm(cast(t), d2, BNN)
    t = t + mm(cast(t), d4, BNN)
    t = t + mm(cast(t), d8, BNN)
    tc = cast(t)
    t = t + mm(tc, mm(cast(jnp.where(off32, g_ab, 0.0)), tc, BNN), BNN)
    tc = cast(t)
    t = t + mm(tc, mm(cast(jnp.where(off64, g_ab, 0.0)), tc, BNN), BNN)

    lv = mm(l_ak, v_s, BNN)
    tu = cast(mm(cast(t), jnp.concatenate([a_s, cast(lv)], axis=2), BNN))
    ru = mm(m_rb, tu, BNN)
    r_hat = cast(r_s + ru[:, :, 0:LANES])
    y_v = ru[:, :, LANES:2 * LANES] + mm(m_rk, v_s, BNN)
    pq = mm(be_s, tu, BTN)
    pp = cast(pq[:, :, 0:LANES])
    q = pq[:, :, LANES:2 * LANES] + mm(ke_s, v_s, BTN)
    wcol = jnp.sum(eye * wend_u, axis=2, keepdims=True)

    s0 = state_ref[...]
    y_parts = []
    for c in range(n_chunks):
        u0, u1 = c * n_pairs, (c + 1) * n_pairs
        s0c = cast(s0)
        y_parts.append(mm(r_hat[u0:u1], s0c, BNN) + y_v[u0:u1])
        s0 = wcol[u0:u1] * s0 + mm(pp[u0:u1], s0c, BNN) + q[u0:u1]
    state_ref[...] = s0
    y_s = jnp.concatenate(y_parts, axis=0)
    y4 = (y_s[:, 0:CHUNK, :] + y_s[:, CHUNK:2 * CHUNK, :]).reshape(n_chunks, n_pairs, CHUNK, LANES)
    y = jnp.concatenate([y4[:, p] for p in range(n_pairs)], axis=-1).reshape(tm, GROUP_W)
    inv_n = 1.0 / HEAD64
    mean = _segsum64(y, ones_bd) * inv_n
    yc = y - mean
    var = _segsum64(yc * yc, ones_bd) * inv_n
    yn = yc * lax.rsqrt(var + RWKV_GN_EPS) * gng_ref[...] + gnb_ref[...]
    o_ref[...] = (yn + bonus) * _silu(g_ref[...])


def _rwkv(p, mu, w0, wup, a0, aup, kkw, kaw, rkw, gng, gnb, tm, passes):
    S = p.shape[0]
    row = lambda n=1, w=GROUP_W: pl.BlockSpec((n, w), lambda i: (0, 0))
    col = lambda off, w=GROUP_W: pl.BlockSpec((tm, w), lambda i: (i, off // w))
    WIN = 3 * GROUP_W + LANES
    return pl.pallas_call(
        functools.partial(_rwkv_kernel, passes=passes),
        grid=(S // tm,),
        in_specs=[
            col(COL_RW_R), col(COL_RW_K), col(COL_RW_V), col(COL_RW_LORA, LANES), col(COL_RW_G),
            row(1, WIN), row(), row(LANES), row(), row(LANES),
            row(), row(), row(), row(), row(),
        ],
        out_specs=pl.BlockSpec((tm, GROUP_W), lambda i: (i, 0)),
        out_shape=jax.ShapeDtypeStruct((S, GROUP_W), F32),
        scratch_shapes=[
            pltpu.VMEM((1, WIN), F32),
            pltpu.VMEM((GROUP_W // LANES, LANES, LANES), F32),
        ],
        compiler_params=pltpu.CompilerParams(
            dimension_semantics=("arbitrary",), vmem_limit_bytes=VMEM_LIMIT),
        name="rwkv7",
    )(p, p, p, p, p, mu, w0, wup, a0, aup, kkw, kaw, rkw, gng, gnb)


def _swa_prepare(kv_ref, kvp_ref, kg_ref):
    tq = kv_ref.shape[0]
    nblk = tq // WINDOW
    kv_all = jnp.concatenate([kvp_ref[...], kv_ref[...]], axis=0)
    k_all = _norm64(kv_all[:, 0:LANES], kg_ref[...]).astype(BF16)
    v_all = kv_all[:, LANES:2 * LANES]
    lo_v = _iota2((WINDOW + tq, LANES), 1) < HEAD64
    v_lo_all = jnp.where(lo_v, v_all, 0.0).astype(BF16)
    v_hi_all = jnp.where(lo_v, 0.0, v_all).astype(BF16)

    window = lambda t: jnp.stack(
        [t[b * WINDOW:(b + 2) * WINDOW, :] for b in range(nblk)], axis=0)

    shape3 = (nblk, 2 * WINDOW, 2 * WINDOW)
    qi = lax.broadcasted_iota(jnp.int32, shape3, 1) % WINDOW
    kj = lax.broadcasted_iota(jnp.int32, shape3, 2)
    blk = lax.broadcasted_iota(jnp.int32, shape3, 0)
    has_prev = (kj >= WINDOW) | (blk > 0) | (pl.program_id(0) > 0)
    mask = (kj > qi) & (kj <= qi + WINDOW) & has_prev
    return window(k_all), window(v_lo_all), window(v_hi_all), mask


def _swa_block(i, prepared, sink_ref, q_ref, g_ref, qg_ref):
    keys, v_lo, v_hi, mask = prepared
    tq = q_ref.shape[0]
    nblk = tq // WINDOW
    c0, c1 = i * LANES, (i + 1) * LANES
    upper_rows = lax.broadcasted_iota(jnp.int32, (1, 2 * WINDOW, 1), 1) >= WINDOW
    lo_q = lax.broadcasted_iota(jnp.int32, (1, WINDOW, LANES), 2) < HEAD64
    qn = _norm64(q_ref[:, c0:c1], qg_ref[...] * (HEAD64 ** -0.5)).reshape(nblk, WINDOW, LANES)
    q_s = jnp.concatenate([jnp.where(lo_q, qn, 0.0), jnp.where(lo_q, 0.0, qn)],
                          axis=1).astype(BF16)
    s = jnp.where(mask, _dot(q_s, keys, BNT), NEG_BIG)
    sink = jnp.where(upper_rows, sink_ref[4 + i], sink_ref[i])
    m = jnp.maximum(jnp.max(s, axis=-1, keepdims=True), sink)
    e = jnp.exp(s - m)
    rden = 1.0 / (jnp.sum(e, axis=-1, keepdims=True) + jnp.exp(sink - m))
    e = e.astype(BF16)
    out = (_dot(e[:, 0:WINDOW, :], v_lo, BNN) * rden[:, 0:WINDOW, :]
           + _dot(e[:, WINDOW:2 * WINDOW, :], v_hi, BNN) * rden[:, WINDOW:2 * WINDOW, :])
    return out.reshape(tq, LANES) * _silu(g_ref[:, c0:c1])


def _memkv_kernel(m_ref, g_ref, w_ref, kg_ref, o_ref, h_ref):
    j = pl.program_id(0)

    @pl.when(j == 0)
    def _():
        x = m_ref[...]
        ms = jnp.mean(x * x, axis=-1, keepdims=True)
        h_ref[...] = (x * lax.rsqrt(ms + EPS) * g_ref[...]).astype(BF16)

    kv = _dot(h_ref[...], w_ref[...])

    @pl.when(j < MEM_HEADS)
    def _():
        ms = jnp.mean(kv * kv, axis=-1, keepdims=True)
        o_ref[...] = kv * lax.rsqrt(ms + EPS) * kg_ref[...]

    @pl.when(j >= MEM_HEADS)
    def _():
        o_ref[...] = kv


def _memkv(mem2, g, w_kv, kg):
    M, D = mem2.shape
    N = w_kv.shape[1]
    return pl.pallas_call(
        _memkv_kernel,
        grid=(N // LANES,),
        in_specs=[
            pl.BlockSpec((M, D), lambda j: (0, 0)),
            pl.BlockSpec((1, D), lambda j: (0, 0)),
            pl.BlockSpec((D, LANES), lambda j: (0, j)),
            pl.BlockSpec((1, LANES), lambda j: (0, 0)),
        ],
        out_specs=pl.BlockSpec((M, LANES), lambda j: (0, j)),
        out_shape=jax.ShapeDtypeStruct((M, N), F32),
        scratch_shapes=[pltpu.VMEM((M, D), BF16)],
        compiler_params=pltpu.CompilerParams(
            dimension_semantics=("arbitrary",), vmem_limit_bytes=VMEM_LIMIT),
        name="memkv",
    )(mem2, g, w_kv, kg)


def _mem_head(h, q_ref, g_ref, kv_ref, qg_ref):
    head = GROUP_W // MEM_HEADS
    c0, c1 = h * head, (h + 1) * head
    q = q_ref[:, c0:c1]
    gain = qg_ref[...] * (head ** -0.5)
    qn = q * lax.rsqrt(jnp.mean(q * q, axis=-1, keepdims=True) + EPS) * gain
    s = _mm(qn, kv_ref[:, c0:c1], NT)
    m = jnp.max(s, axis=-1, keepdims=True)
    e = jnp.exp(s - m)
    rden = 1.0 / jnp.sum(e, axis=-1, keepdims=True)
    out = _mm(e, kv_ref[:, GROUP_W + c0:GROUP_W + c1]) * rden
    return out * _silu(g_ref[:, c0:c1])


def _post_kernel(sink_ref, x_ref, orw_ref, lx_ref, lg_ref, sq_ref, skv_ref, skvp_ref, sg_ref,
                 mq_ref, mg_ref, cw_ref, cb_ref, wa_ref, ba_ref, wx_ref, bx_ref, lam_ref,
                 sqg_ref, skg_ref, mkv_ref, mqg_ref, w_ref, y_ref, xbuf_ref, h_ref):
    @pl.when(pl.program_id(0) == 0)
    def _():
        xbuf_ref[0:8, :] = jnp.zeros((8, GROUP_W), F32)
        h_ref[...] = jnp.zeros_like(h_ref)

    n_stage = GROUP_W // LANES
    piece = y_ref.shape[1] // n_stage

    def project(o16, group, j):
        return _dot(o16, w_ref[group * GROUP_W:(group + 1) * GROUP_W, j * piece:(j + 1) * piece])

    def accumulate(stage_fn, o16, group, first=False):
        cols = []
        for j in range(n_stage):
            cols.append(stage_fn(j))
            n0, n1 = j * piece, (j + 1) * piece
            base = x_ref[:, n0:n1] if first else y_ref[:, n0:n1]
            y_ref[:, n0:n1] = base + project(o16, group, j)
        return jnp.concatenate(cols, axis=1).astype(BF16)

    lru = functools.partial(_lru_block, x_ref=lx_ref, g_ref=lg_ref, cw_ref=cw_ref, cb_ref=cb_ref,
                            wa_ref=wa_ref, ba_ref=ba_ref, wx_ref=wx_ref, bx_ref=bx_ref,
                            lam_ref=lam_ref, xbuf_ref=xbuf_ref, h_ref=h_ref)
    o_lru = accumulate(lru, orw_ref[...].astype(BF16), 1, first=True)
    prepared = _swa_prepare(skv_ref, skvp_ref, skg_ref)
    swa = functools.partial(_swa_block, prepared=prepared, sink_ref=sink_ref, q_ref=sq_ref,
                            g_ref=sg_ref, qg_ref=sqg_ref)
    o_swa = accumulate(swa, o_lru, 0)
    mem = functools.partial(_mem_head, q_ref=mq_ref, g_ref=mg_ref, kv_ref=mkv_ref, qg_ref=mqg_ref)
    o_mem = accumulate(mem, o_swa, 2)
    for j in range(n_stage):
        n0, n1 = j * piece, (j + 1) * piece
        y_ref[:, n0:n1] = y_ref[:, n0:n1] + project(o_mem, 3, j)


def _post(x2, p, o_rw, sinks, lru_prm, swa_prm, mem_prm, w_out, tm):
    S, D = x2.shape
    nblk = tm // WINDOW
    col = lambda off, w=GROUP_W: pl.BlockSpec((tm, w), lambda i: (i, off // w))
    const = lambda shape: pl.BlockSpec(shape, lambda i: (0,) * len(shape))
    cw, cb, wa_bd, ba, wx_bd, bx, lam = lru_prm
    qg2, kg2 = swa_prm
    kv, mqg = mem_prm
    return pl.pallas_call(
        _post_kernel,
        grid=(S // tm,),
        in_specs=[
            pl.BlockSpec(memory_space=pltpu.SMEM),
            pl.BlockSpec((tm, D), lambda i: (i, 0)),
            pl.BlockSpec((tm, GROUP_W), lambda i: (i, 0)),
            col(COL_LRU_X), col(COL_LRU_G),
            col(COL_SWA_Q), col(COL_SWA_KV, 2 * LANES),
            pl.BlockSpec((WINDOW, 2 * LANES),
                         lambda i: (jnp.maximum(i * nblk - 1, 0), COL_SWA_KV // (2 * LANES))),
            col(COL_SWA_G), col(COL_MEM_Q), col(COL_MEM_G),
            const((CONV_W, GROUP_W)), const((1, GROUP_W)),
            const((GROUP_W, GROUP_W)), const((1, GROUP_W)),
            const((GROUP_W, GROUP_W)), const((1, GROUP_W)), const((1, GROUP_W)),
            const((1, LANES)), const((1, LANES)),
            const((N_MEM, 2 * GROUP_W)), const((1, GROUP_W // MEM_HEADS)),
            pl.BlockSpec((4 * GROUP_W, D), lambda i: (0, 0), pipeline_mode=pl.Buffered(1)),
        ],
        out_specs=pl.BlockSpec((tm, D), lambda i: (i, 0)),
        out_shape=jax.ShapeDtypeStruct((S, D), F32),
        scratch_shapes=[pltpu.VMEM((tm + 8, GROUP_W), F32), pltpu.VMEM((1, GROUP_W), F32)],
        compiler_params=pltpu.CompilerParams(
            dimension_semantics=("arbitrary",), vmem_limit_bytes=VMEM_LIMIT),
        name="post",
    )(sinks, x2, o_rw, p, p, p, p, p, p, p, p, cw, cb, wa_bd, ba, wx_bd, bx, lam,
      qg2, kg2, kv, mqg, w_out)


SWA_PERM = (0, 4, 1, 5, 2, 6, 3, 7)


def _perm_heads(t, axis):
    parts = jnp.split(t, 8, axis=axis)
    return jnp.concatenate([parts[h] for h in SWA_PERM], axis=axis)


def _tail_in_weights(w):
    D = w.shape[0]
    sizes = (LORA, LORA, 512, 512, 128, 128, 512, 512, 512)
    offs = [IN_HEAD]
    for s in sizes:
        offs.append(offs[-1] + s)
    (rw_wd, rw_ad, rw_g, sq, sk, sv, sg, mq, mg) = [
        w[:, offs[n]:offs[n + 1]] for n in range(len(sizes))]
    z = lambda n: jnp.zeros((D, n), w.dtype)
    cols = [rw_g, _perm_heads(sq, 1), _perm_heads(sg, 1), mq, mg,
            sk, sv, rw_wd, rw_ad, z(LANES - 2 * LORA), z(LANES)]
    out = jnp.concatenate(cols, axis=1)
    assert out.shape[1] == IN_PAD - IN_HEAD
    return out


def _block_diag(w):
    nb, bw, _ = w.shape
    eye = jnp.eye(nb, dtype=w.dtype)
    return (eye[:, None, :, None] * w[:, :, None, :]).reshape(nb * bw, nb * bw)


def _layer(x2, mem2, prm, w_in_all, layer, tiles, passes):
    (norm_g, w_in, conv_w, conv_b, lru_wa, lru_ba, lru_wx, lru_bx, lru_lambda,
     rw_mu, rw_w0, rw_w_up, rw_a0, rw_a_up, rw_k_k, rw_k_a, rw_r_k, rw_gn_g, rw_gn_b,
     swa_q_g, swa_k_g, swa_sinks, mem_norm_g, w_mem_kv, mem_q_g, mem_k_g, w_out) = prm
    row = lambda t: t.reshape(1, -1)

    p = _inproj(x2, row(norm_g), w_in_all, layer, _tail_in_weights(w_in_all[layer]),
                tiles["inproj"])

    mu_pad = jnp.concatenate([rw_mu, jnp.zeros((LANES - 2 * LORA,), F32)]).reshape(1, -1)
    zpad = jnp.zeros((LANES - LORA, GROUP_W), F32)
    wup_pad = jnp.concatenate([rw_w_up, zpad], axis=0)
    aup_pad = jnp.concatenate([jnp.zeros((LORA, GROUP_W), F32), rw_a_up,
                               jnp.zeros((LANES - 2 * LORA, GROUP_W), F32)], axis=0)
    o_rw = _rwkv(p, mu_pad, row(rw_w0), wup_pad, row(rw_a0), aup_pad, row(rw_k_k), row(rw_k_a),
                 row(rw_r_k), row(rw_gn_g), row(rw_gn_b), tiles["rwkv"], passes)

    kv = _memkv(mem2, row(mem_norm_g), w_mem_kv.astype(BF16), row(mem_k_g))
    two = lambda t: jnp.concatenate([t, t]).reshape(1, LANES)
    w_out_p = jnp.concatenate(
        [w_out[0:2 * GROUP_W], _perm_heads(w_out[2 * GROUP_W:3 * GROUP_W], 0), w_out[3 * GROUP_W:]],
        axis=0).astype(BF16)
    lru_prm = (conv_w, row(conv_b), _block_diag(lru_wa).astype(BF16), row(lru_ba),
               _block_diag(lru_wx).astype(BF16), row(lru_bx), row(lru_lambda))
    return _post(x2, p, o_rw, swa_sinks, lru_prm, (two(swa_q_g), two(swa_k_g)),
                 (kv, row(mem_q_g)), w_out_p, tiles["post"])


def _tiles(S):
    pick = lambda t: t if S % t == 0 else S
    return {"inproj": pick(256), "rwkv": pick(256), "post": pick(512)}


def kernel(x, mem, norm_g, w_in, conv_w, conv_b, lru_wa, lru_ba, lru_wx, lru_bx, lru_lambda, rw_mu, rw_w0, rw_w_up, rw_a0, rw_a_up, rw_k_k, rw_k_a, rw_r_k, rw_gn_g, rw_gn_b, swa_q_g, swa_k_g, swa_sinks, mem_norm_g, w_mem_kv, mem_q_g, mem_k_g, w_out):
    B, S, D = x.shape
    assert B == 1
    params = (norm_g, w_in, conv_w, conv_b, lru_wa, lru_ba, lru_wx, lru_bx, lru_lambda,
              rw_mu, rw_w0, rw_w_up, rw_a0, rw_a_up, rw_k_k, rw_k_a, rw_r_k, rw_gn_g, rw_gn_b,
              swa_q_g, swa_k_g, swa_sinks, mem_norm_g, w_mem_kv, mem_q_g, mem_k_g, w_out)
    x2 = x.reshape(S, D)
    mem2 = mem.reshape(mem.shape[1], D)
    tiles = _tiles(S)
    w_in_all = w_in.astype(BF16)
    for l in range(norm_g.shape[0]):
        x2 = _layer(x2, mem2, tuple(t[l] for t in params), w_in_all, l, tiles, passes=1)
    return x2.reshape(B, S, D)
```

```python
import functools

import jax
import jax.numpy as jnp
from jax import lax
from jax.experimental import pallas as pl
from jax.experimental.pallas import tpu as pltpu

F32 = jnp.float32
BF16 = jnp.bfloat16

LANES = 128
SUBLANES = 8
GROUP_W = 512
HEAD64 = 64
CHUNK = 64
CONV_W = 4
LRU_C = 8.0
LORA = 32
EPS = 1e-6
RWKV_GN_EPS = 64e-5
WINDOW = 128
N_MEM = 256
MEM_HEADS = 4
NEG_BIG = -1e30
VMEM_LIMIT = 56 * 1024 * 1024

COL_LRU_X, COL_LRU_G = 0, 512
COL_RW_R, COL_RW_K, COL_RW_V, COL_RW_G = 1024, 1536, 2048, 2560
COL_SWA_Q, COL_SWA_G = 3072, 3584
COL_MEM_Q, COL_MEM_G = 4096, 4608
COL_SWA_KV, COL_RW_LORA = 5120, 5376
IN_PAD = 5632
IN_HEAD = 2560

NN = (((1,), (0,)), ((), ()))
NT = (((1,), (1,)), ((), ()))
TN = (((0,), (0,)), ((), ()))
BNN = (((2,), (1,)), ((0,), (0,)))
BNT = (((2,), (2,)), ((0,), (0,)))
BTN = (((1,), (1,)), ((0,), (0,)))


def _dot(a, b, dims=NN):
    return lax.dot_general(a, b, dims, preferred_element_type=F32)


def _split(a):
    hi = a.astype(BF16)
    lo = (a - hi.astype(F32)).astype(BF16)
    return hi, lo


def _mm(a, b, dims=NN, passes=1):
    if passes == 1:
        return _dot(a.astype(BF16), b.astype(BF16), dims)
    ah, al = _split(a)
    bh, bl = _split(b)
    return _dot(ah, bh, dims) + (_dot(ah, bl, dims) + _dot(al, bh, dims))


def _mm_exact_rhs(a, b_bf16):
    ah, al = _split(a)
    return _dot(ah, b_bf16) + _dot(al, b_bf16)


def _iota2(shape, axis):
    return lax.broadcasted_iota(jnp.int32, shape, axis)


def _head_ones():
    r = _iota2((LANES, LANES), 0) // HEAD64
    c = _iota2((LANES, LANES), 1) // HEAD64
    return (r == c).astype(BF16)


def _segsum64(x, ones_bd):
    n, w = x.shape
    cols = [_mm_exact_rhs(x[:, j:j + LANES], ones_bd) for j in range(0, w, LANES)]
    return cols[0] if len(cols) == 1 else jnp.concatenate(cols, axis=1)


def _norm64(x, gain):
    ss = _segsum64(x * x, _head_ones())
    return x * lax.rsqrt(ss * (1.0 / HEAD64) + EPS) * gain


def _softplus(z):
    return jnp.maximum(z, 0.0) + jnp.log1p(jnp.exp(-jnp.abs(z)))


def _sigmoid(z):
    return 0.5 * jnp.tanh(0.5 * z) + 0.5


def _silu(z):
    return z * _sigmoid(z)


def _inproj_kernel(x_ref, g_ref, w_ref, o_ref):
    x = x_ref[...]
    ms = jnp.mean(x * x, axis=-1, keepdims=True)
    h = (x * lax.rsqrt(ms + EPS) * g_ref[...]).astype(BF16)
    o_ref[...] = _dot(h, w_ref[...])


def _inproj(x2, g, w_pad_all, layer, tm):
    S, D = x2.shape
    return pl.pallas_call(
        _inproj_kernel,
        grid=(S // tm,),
        in_specs=[
            pl.BlockSpec((tm, D), lambda i: (i, 0)),
            pl.BlockSpec((1, D), lambda i: (0, 0)),
            pl.BlockSpec((None, D, IN_PAD), lambda i: (layer, 0, 0),
                         pipeline_mode=pl.Buffered(1)),
        ],
        out_specs=pl.BlockSpec((tm, IN_PAD), lambda i: (i, 0)),
        out_shape=jax.ShapeDtypeStruct((S, IN_PAD), F32),
        compiler_params=pltpu.CompilerParams(
            dimension_semantics=("arbitrary",), vmem_limit_bytes=VMEM_LIMIT),
        name="inproj",
    )(x2, g, w_pad_all)


def _lru_block(j, x_ref, g_ref, cw_ref, cb_ref, wa_ref, ba_ref, wx_ref, bx_ref, lam_ref,
               xbuf_ref, h_ref):
    tm = x_ref.shape[0]
    c0, c1 = j * LANES, (j + 1) * LANES
    x = x_ref[:, c0:c1]
    xbuf_ref[8:8 + tm, c0:c1] = x
    conv = cb_ref[:, c0:c1] + cw_ref[CONV_W - 1:CONV_W, c0:c1] * x
    for t in range(CONV_W - 1):
        off = 8 - (CONV_W - 1) + t
        conv = conv + cw_ref[t:t + 1, c0:c1] * xbuf_ref[off:off + tm, c0:c1]
    xbuf_ref[0:8, c0:c1] = x[tm - 8:tm, :]

    cb16 = conv.astype(BF16)
    r = _sigmoid(_dot(cb16, wa_ref[c0:c1, c0:c1]) + ba_ref[:, c0:c1])
    ig = _sigmoid(_dot(cb16, wx_ref[c0:c1, c0:c1]) + bx_ref[:, c0:c1])
    log_a = (-LRU_C) * r * _softplus(-lam_ref[:, c0:c1])
    a = jnp.exp(log_a)
    m2 = jnp.maximum(-jnp.tanh(log_a) * (a * a + 1.0), 1e-12)
    u = (m2 * lax.rsqrt(m2)) * (ig * conv)

    n_slabs = tm // SUBLANES
    a3 = a.reshape(n_slabs, SUBLANES, LANES)
    u3 = u.reshape(n_slabs, SUBLANES, LANES)
    sub = lax.broadcasted_iota(jnp.int32, (1, SUBLANES, LANES), 1)
    d = 1
    while d < SUBLANES:
        keep = sub >= d
        a_sh = jnp.where(keep, pltpu.roll(a3, d, 1), 1.0)
        u_sh = jnp.where(keep, pltpu.roll(u3, d, 1), 0.0)
        u3 = u3 + a3 * u_sh
        a3 = a3 * a_sh
        d *= 2
    h = h_ref[:, c0:c1]
    slabs = []
    for s in range(n_slabs):
        h_s = u3[s] + a3[s] * h
        slabs.append(h_s)
        h = h_s[SUBLANES - 1:SUBLANES, :]
    h_ref[:, c0:c1] = h
    return jnp.concatenate(slabs, axis=0) * _silu(g_ref[:, c0:c1])


def _rwkv_kernel(r_ref, k_ref, v_ref, l_ref, g_ref, mu_ref, w0_ref, wup_ref, a0_ref, aup_ref,
                 kk_ref, ka_ref, rk_ref, gng_ref, gnb_ref, o_ref,
                 prev_ref, state_ref, *, passes):
    tm = r_ref.shape[0]
    n_chunks = tm // CHUNK
    n_pairs = GROUP_W // LANES
    WIN = 3 * GROUP_W + LANES

    @pl.when(pl.program_id(0) == 0)
    def _():
        prev_ref[...] = jnp.zeros_like(prev_ref)
        state_ref[...] = jnp.zeros_like(state_ref)

    row_t = _iota2((tm, 1), 0)

    def shift_lerp(ref, c0, c1):
        x = ref[...]
        xs = jnp.where(row_t == 0, prev_ref[:, c0:c1], pltpu.roll(x, 1, 0))
        prev_ref[:, c0:c1] = x[tm - 1:tm, :]
        return x + (xs - x) * mu_ref[:, c0:c1]

    r = shift_lerp(r_ref, 0, GROUP_W)
    k = shift_lerp(k_ref, GROUP_W, 2 * GROUP_W)
    v = shift_lerp(v_ref, 2 * GROUP_W, 3 * GROUP_W)
    lo_in = shift_lerp(l_ref, 3 * GROUP_W, WIN)

    lane_l = _iota2((tm, LANES), 1)
    lora = jnp.where(lane_l < LORA, jnp.tanh(lo_in), lo_in)
    w = -_softplus(-(w0_ref[...] + _mm(lora, wup_ref[...], passes=3))) - 0.5
    lw = -jnp.exp(w)
    a = _sigmoid(a0_ref[...] + _mm(lora, aup_ref[...], passes=3))

    ones_bd = _head_ones()
    kk = k * kk_ref[...]
    kk = kk / jnp.maximum(jnp.sqrt(_segsum64(kk * kk, ones_bd)), 1e-12)
    k2 = k * (1.0 + (a - 1.0) * ka_ref[...])
    b = kk * a
    bonus = _segsum64(r * k2 * rk_ref[...], ones_bd) * v

    cl = lw
    rows_c = _iota2((tm, GROUP_W), 0) % CHUNK
    d = 1
    while d < CHUNK:
        cl = cl + jnp.where(rows_c >= d, pltpu.roll(cl, d, 0), 0.0)
        d *= 2

    to3 = lambda t: t.reshape(n_chunks, CHUNK, GROUP_W)
    cl3 = to3(cl)
    cl_end = cl3[:, CHUNK - 1:CHUNK, :]
    e_inv = jnp.exp(-cl3)
    e_end = jnp.exp(cl_end - cl3)
    w_end = jnp.exp(cl_end)
    kk3, b3, k23 = to3(kk), to3(b), to3(k2)

    lo_lane = lax.broadcasted_iota(jnp.int32, (1, CHUNK, LANES), 2) < HEAD64
    n_units = n_chunks * n_pairs

    def stack(t3, dtype):
        units = []
        for p in range(n_pairs):
            x = t3[:, :, p * LANES:(p + 1) * LANES]
            units.append(jnp.concatenate(
                [jnp.where(lo_lane, x, 0.0), jnp.where(lo_lane, 0.0, x)], axis=1).astype(dtype))
        return jnp.stack(units, axis=1).reshape(n_units, 2 * CHUNK, LANES)

    op_dtype = BF16 if passes == 1 else F32
    cast = lambda t: t.astype(op_dtype)
    mm = functools.partial(_mm, passes=passes)

    a_s = stack(-kk3 * jnp.exp(cl3 - to3(lw)), op_dtype)
    r_s = stack(to3(r) * jnp.exp(cl3), F32)
    b_s = stack(b3 * e_inv, op_dtype)
    k_s = stack(k23 * e_inv, op_dtype)
    be_s = stack(b3 * e_end, op_dtype)
    ke_s = stack(k23 * e_end, op_dtype)
    v_s = stack(to3(v), op_dtype)
    wend_u = jnp.stack([w_end[:, :, p * LANES:(p + 1) * LANES] for p in range(n_pairs)],
                       axis=1).reshape(n_units, 1, LANES)

    ri = lax.broadcasted_iota(jnp.int32, (1, LANES, LANES), 1)
    ci = lax.broadcasted_iota(jnp.int32, (1, LANES, LANES), 2)
    same_head = (ri // CHUNK) == (ci // CHUNK)
    strict = same_head & (ci < ri)
    incl = same_head & (ci <= ri)
    diag16 = strict & ((ri // 16) == (ci // 16))
    off32 = strict & ((ri // 32) == (ci // 32)) & ((ri // 16) != (ci // 16))
    off64 = strict & ((ri // 32) != (ci // 32))
    eye = (ri == ci).astype(F32)

    g = mm(jnp.concatenate([a_s, cast(r_s)], axis=1), jnp.concatenate([b_s, k_s], axis=1), BNT)
    g_ab, g_ak = g[:, 0:LANES, 0:LANES], g[:, 0:LANES, LANES:2 * LANES]
    g_rb, g_rk = g[:, LANES:2 * LANES, 0:LANES], g[:, LANES:2 * LANES, LANES:2 * LANES]
    l_ak = cast(jnp.where(strict, g_ak, 0.0))
    m_rb = cast(jnp.where(incl, g_rb, 0.0))
    m_rk = cast(jnp.where(incl, g_rk, 0.0))

    d1 = jnp.where(diag16, g_ab, 0.0)
    d1c = cast(d1)
    d2 = cast(mm(d1c, d1c, BNN))
    d4 = cast(mm(d2, d2, BNN))
    d8 = mm(d4, d4, BNN)
    t = eye + d1
    t = t + mm(cast(t), d2, BNN)
    t = t + mm(cast(t), d4, BNN)
    t = t + mm(cast(t), d8, BNN)
    tc = cast(t)
    t = t + mm(tc, mm(cast(jnp.where(off32, g_ab, 0.0)), tc, BNN), BNN)
    tc = cast(t)
    t = t + mm(tc, mm(cast(jnp.where(off64, g_ab, 0.0)), tc, BNN), BNN)

    lv = mm(l_ak, v_s, BNN)
    tu = cast(mm(cast(t), jnp.concatenate([a_s, cast(lv)], axis=2), BNN))
    ru = mm(m_rb, tu, BNN)
    r_hat = cast(r_s + ru[:, :, 0:LANES])
    y_v = ru[:, :, LANES:2 * LANES] + mm(m_rk, v_s, BNN)
    pq = mm(be_s, tu, BTN)
    pp = cast(pq[:, :, 0:LANES])
    q = pq[:, :, LANES:2 * LANES] + mm(ke_s, v_s, BTN)
    wcol = jnp.sum(eye * wend_u, axis=2, keepdims=True)

    s0 = state_ref[...]
    y_parts = []
    for c in range(n_chunks):
        u0, u1 = c * n_pairs, (c + 1) * n_pairs
        s0c = cast(s0)
        y_parts.append(mm(r_hat[u0:u1], s0c, BNN) + y_v[u0:u1])
        s0 = wcol[u0:u1] * s0 + mm(pp[u0:u1], s0c, BNN) + q[u0:u1]
    state_ref[...] = s0
    y_s = jnp.concatenate(y_parts, axis=0)
    y4 = (y_s[:, 0:CHUNK, :] + y_s[:, CHUNK:2 * CHUNK, :]).reshape(n_chunks, n_pairs, CHUNK, LANES)
    y = jnp.concatenate([y4[:, p] for p in range(n_pairs)], axis=-1).reshape(tm, GROUP_W)
    inv_n = 1.0 / HEAD64
    mean = _segsum64(y, ones_bd) * inv_n
    yc = y - mean
    var = _segsum64(yc * yc, ones_bd) * inv_n
    yn = yc * lax.rsqrt(var + RWKV_GN_EPS) * gng_ref[...] + gnb_ref[...]
    o_ref[...] = (yn + bonus) * _silu(g_ref[...])


def _rwkv(p, mu, w0, wup, a0, aup, kkw, kaw, rkw, gng, gnb, tm, passes):
    S = p.shape[0]
    row = lambda n=1, w=GROUP_W: pl.BlockSpec((n, w), lambda i: (0, 0))
    col = lambda off, w=GROUP_W: pl.BlockSpec((tm, w), lambda i: (i, off // w))
    WIN = 3 * GROUP_W + LANES
    return pl.pallas_call(
        functools.partial(_rwkv_kernel, passes=passes),
        grid=(S // tm,),
        in_specs=[
            col(COL_RW_R), col(COL_RW_K), col(COL_RW_V), col(COL_RW_LORA, LANES), col(COL_RW_G),
            row(1, WIN), row(), row(LANES), row(), row(LANES),
            row(), row(), row(), row(), row(),
        ],
        out_specs=pl.BlockSpec((tm, GROUP_W), lambda i: (i, 0)),
        out_shape=jax.ShapeDtypeStruct((S, GROUP_W), F32),
        scratch_shapes=[
            pltpu.VMEM((1, WIN), F32),
            pltpu.VMEM((GROUP_W // LANES, LANES, LANES), F32),
        ],
        compiler_params=pltpu.CompilerParams(
            dimension_semantics=("arbitrary",), vmem_limit_bytes=VMEM_LIMIT),
        name="rwkv7",
    )(p, p, p, p, p, mu, w0, wup, a0, aup, kkw, kaw, rkw, gng, gnb)


def _swa_prepare(kv_ref, kvp_ref, kg_ref):
    tq = kv_ref.shape[0]
    nblk = tq // WINDOW
    kv_all = jnp.concatenate([kvp_ref[...], kv_ref[...]], axis=0)
    k_all = _norm64(kv_all[:, 0:LANES], kg_ref[...]).astype(BF16)
    v_all = kv_all[:, LANES:2 * LANES]
    lo_v = _iota2((WINDOW + tq, LANES), 1) < HEAD64
    v_lo_all = jnp.where(lo_v, v_all, 0.0).astype(BF16)
    v_hi_all = jnp.where(lo_v, 0.0, v_all).astype(BF16)

    window = lambda t: jnp.stack(
        [t[b * WINDOW:(b + 2) * WINDOW, :] for b in range(nblk)], axis=0)

    shape3 = (nblk, 2 * WINDOW, 2 * WINDOW)
    qi = lax.broadcasted_iota(jnp.int32, shape3, 1) % WINDOW
    kj = lax.broadcasted_iota(jnp.int32, shape3, 2)
    blk = lax.broadcasted_iota(jnp.int32, shape3, 0)
    has_prev = (kj >= WINDOW) | (blk > 0) | (pl.program_id(0) > 0)
    mask = (kj > qi) & (kj <= qi + WINDOW) & has_prev
    return window(k_all), window(v_lo_all), window(v_hi_all), mask


def _swa_block(i, prepared, sink_ref, q_ref, g_ref, qg_ref):
    keys, v_lo, v_hi, mask = prepared
    tq = q_ref.shape[0]
    nblk = tq // WINDOW
    c0, c1 = i * LANES, (i + 1) * LANES
    upper_rows = lax.broadcasted_iota(jnp.int32, (1, 2 * WINDOW, 1), 1) >= WINDOW
    lo_q = lax.broadcasted_iota(jnp.int32, (1, WINDOW, LANES), 2) < HEAD64
    qn = _norm64(q_ref[:, c0:c1], qg_ref[...] * (HEAD64 ** -0.5)).reshape(nblk, WINDOW, LANES)
    q_s = jnp.concatenate([jnp.where(lo_q, qn, 0.0), jnp.where(lo_q, 0.0, qn)],
                          axis=1).astype(BF16)
    s = jnp.where(mask, _dot(q_s, keys, BNT), NEG_BIG)
    sink = jnp.where(upper_rows, sink_ref[4 + i], sink_ref[i])
    m = jnp.maximum(jnp.max(s, axis=-1, keepdims=True), sink)
    e = jnp.exp(s - m)
    rden = 1.0 / (jnp.sum(e, axis=-1, keepdims=True) + jnp.exp(sink - m))
    e = e.astype(BF16)
    out = (_dot(e[:, 0:WINDOW, :], v_lo, BNN) * rden[:, 0:WINDOW, :]
           + _dot(e[:, WINDOW:2 * WINDOW, :], v_hi, BNN) * rden[:, WINDOW:2 * WINDOW, :])
    return out.reshape(tq, LANES) * _silu(g_ref[:, c0:c1])


def _memkv_kernel(m_ref, g_ref, w_ref, kg_ref, o_ref, h_ref):
    j = pl.program_id(0)

    @pl.when(j == 0)
    def _():
        x = m_ref[...]
        ms = jnp.mean(x * x, axis=-1, keepdims=True)
        h_ref[...] = (x * lax.rsqrt(ms + EPS) * g_ref[...]).astype(BF16)

    kv = _dot(h_ref[...], w_ref[...])

    @pl.when(j < MEM_HEADS)
    def _():
        ms = jnp.mean(kv * kv, axis=-1, keepdims=True)
        o_ref[...] = kv * lax.rsqrt(ms + EPS) * kg_ref[...]

    @pl.when(j >= MEM_HEADS)
    def _():
        o_ref[...] = kv


def _memkv(mem2, g, w_kv, kg):
    M, D = mem2.shape
    N = w_kv.shape[1]
    return pl.pallas_call(
        _memkv_kernel,
        grid=(N // LANES,),
        in_specs=[
            pl.BlockSpec((M, D), lambda j: (0, 0)),
            pl.BlockSpec((1, D), lambda j: (0, 0)),
            pl.BlockSpec((D, LANES), lambda j: (0, j)),
            pl.BlockSpec((1, LANES), lambda j: (0, 0)),
        ],
        out_specs=pl.BlockSpec((M, LANES), lambda j: (0, j)),
        out_shape=jax.ShapeDtypeStruct((M, N), F32),
        scratch_shapes=[pltpu.VMEM((M, D), BF16)],
        compiler_params=pltpu.CompilerParams(
            dimension_semantics=("arbitrary",), vmem_limit_bytes=VMEM_LIMIT),
        name="memkv",
    )(mem2, g, w_kv, kg)


def _mem_head(h, q_ref, g_ref, kv_ref, qg_ref):
    head = GROUP_W // MEM_HEADS
    c0, c1 = h * head, (h + 1) * head
    q = q_ref[:, c0:c1]
    gain = qg_ref[...] * (head ** -0.5)
    qn = q * lax.rsqrt(jnp.mean(q * q, axis=-1, keepdims=True) + EPS) * gain
    s = _mm(qn, kv_ref[:, c0:c1], NT)
    m = jnp.max(s, axis=-1, keepdims=True)
    e = jnp.exp(s - m)
    rden = 1.0 / jnp.sum(e, axis=-1, keepdims=True)
    out = _mm(e, kv_ref[:, GROUP_W + c0:GROUP_W + c1]) * rden
    return out * _silu(g_ref[:, c0:c1])


def _post_kernel(sink_ref, x_ref, orw_ref, lx_ref, lg_ref, sq_ref, skv_ref, skvp_ref, sg_ref,
                 mq_ref, mg_ref, cw_ref, cb_ref, wa_ref, ba_ref, wx_ref, bx_ref, lam_ref,
                 sqg_ref, skg_ref, mkv_ref, mqg_ref, w_ref, y_ref, xbuf_ref, h_ref):
    @pl.when(pl.program_id(0) == 0)
    def _():
        xbuf_ref[0:8, :] = jnp.zeros((8, GROUP_W), F32)
        h_ref[...] = jnp.zeros_like(h_ref)

    n_stage = GROUP_W // LANES
    piece = y_ref.shape[1] // n_stage

    def project(o16, group, j):
        return _dot(o16, w_ref[group * GROUP_W:(group + 1) * GROUP_W, j * piece:(j + 1) * piece])

    def accumulate(stage_fn, o16, group, first=False):
        cols = []
        for j in range(n_stage):
            cols.append(stage_fn(j))
            n0, n1 = j * piece, (j + 1) * piece
            base = x_ref[:, n0:n1] if first else y_ref[:, n0:n1]
            y_ref[:, n0:n1] = base + project(o16, group, j)
        return jnp.concatenate(cols, axis=1).astype(BF16)

    lru = functools.partial(_lru_block, x_ref=lx_ref, g_ref=lg_ref, cw_ref=cw_ref, cb_ref=cb_ref,
                            wa_ref=wa_ref, ba_ref=ba_ref, wx_ref=wx_ref, bx_ref=bx_ref,
                            lam_ref=lam_ref, xbuf_ref=xbuf_ref, h_ref=h_ref)
    o_lru = accumulate(lru, orw_ref[...].astype(BF16), 1, first=True)
    prepared = _swa_prepare(skv_ref, skvp_ref, skg_ref)
    swa = functools.partial(_swa_block, prepared=prepared, sink_ref=sink_ref, q_ref=sq_ref,
                            g_ref=sg_ref, qg_ref=sqg_ref)
    o_swa = accumulate(swa, o_lru, 0)
    mem = functools.partial(_mem_head, q_ref=mq_ref, g_ref=mg_ref, kv_ref=mkv_ref, qg_ref=mqg_ref)
    o_mem = accumulate(mem, o_swa, 2)
    for j in range(n_stage):
        n0, n1 = j * piece, (j + 1) * piece
        y_ref[:, n0:n1] = y_ref[:, n0:n1] + project(o_mem, 3, j)


def _post(x2, p, o_rw, sinks, lru_prm, swa_prm, mem_prm, w_out, tm):
    S, D = x2.shape
    nblk = tm // WINDOW
    col = lambda off, w=GROUP_W: pl.BlockSpec((tm, w), lambda i: (i, off // w))
    const = lambda shape: pl.BlockSpec(shape, lambda i: (0,) * len(shape))
    cw, cb, wa_bd, ba, wx_bd, bx, lam = lru_prm
    qg2, kg2 = swa_prm
    kv, mqg = mem_prm
    return pl.pallas_call(
        _post_kernel,
        grid=(S // tm,),
        in_specs=[
            pl.BlockSpec(memory_space=pltpu.SMEM),
            pl.BlockSpec((tm, D), lambda i: (i, 0)),
            pl.BlockSpec((tm, GROUP_W), lambda i: (i, 0)),
            col(COL_LRU_X), col(COL_LRU_G),
            col(COL_SWA_Q), col(COL_SWA_KV, 2 * LANES),
            pl.BlockSpec((WINDOW, 2 * LANES),
                         lambda i: (jnp.maximum(i * nblk - 1, 0), COL_SWA_KV // (2 * LANES))),
            col(COL_SWA_G), col(COL_MEM_Q), col(COL_MEM_G),
            const((CONV_W, GROUP_W)), const((1, GROUP_W)),
            const((GROUP_W, GROUP_W)), const((1, GROUP_W)),
            const((GROUP_W, GROUP_W)), const((1, GROUP_W)), const((1, GROUP_W)),
            const((1, LANES)), const((1, LANES)),
            const((N_MEM, 2 * GROUP_W)), const((1, GROUP_W // MEM_HEADS)),
            pl.BlockSpec((4 * GROUP_W, D), lambda i: (0, 0), pipeline_mode=pl.Buffered(1)),
        ],
        out_specs=pl.BlockSpec((tm, D), lambda i: (i, 0)),
        out_shape=jax.ShapeDtypeStruct((S, D), F32),
        scratch_shapes=[pltpu.VMEM((tm + 8, GROUP_W), F32), pltpu.VMEM((1, GROUP_W), F32)],
        compiler_params=pltpu.CompilerParams(
            dimension_semantics=("arbitrary",), vmem_limit_bytes=VMEM_LIMIT),
        name="post",
    )(sinks, x2, o_rw, p, p, p, p, p, p, p, p, cw, cb, wa_bd, ba, wx_bd, bx, lam,
      qg2, kg2, kv, mqg, w_out)


SWA_PERM = (0, 4, 1, 5, 2, 6, 3, 7)


def _perm_heads(t, axis):
    parts = jnp.split(t, 8, axis=axis)
    return jnp.concatenate([parts[h] for h in SWA_PERM], axis=axis)


def _wprep_kernel(w_ref, o_ref):
    half = lax.broadcasted_iota(jnp.int32, (w_ref.shape[0], LANES), 1) < HEAD64

    def cols(start, width=LANES):
        return w_ref[:, start:start + width]

    def put(dst, val):
        o_ref[:, dst:dst + val.shape[1]] = val.astype(BF16)

    put(0, cols(0, IN_HEAD))
    src = IN_HEAD + 2 * LORA
    put(COL_RW_G, cols(src, GROUP_W))
    src_q = src + GROUP_W
    src_kv = src_q + GROUP_W
    src_g = src_kv + 2 * LANES
    for base, dst in ((src_q, COL_SWA_Q), (src_g, COL_SWA_G)):
        for i in range(GROUP_W // LANES):
            lo = cols(base + HEAD64 * i)
            hi = cols(base + HEAD64 * (3 + i))
            put(dst + i * LANES, jnp.where(half, lo, hi))
    put(COL_MEM_Q, cols(src_g + GROUP_W, 2 * GROUP_W))
    put(COL_SWA_KV, cols(src_kv, 2 * LANES))
    put(COL_RW_LORA, jnp.where(half, cols(IN_HEAD), 0.0))
    put(COL_RW_LORA + LANES, jnp.zeros((w_ref.shape[0], LANES), F32))


def _wprep(w_in, tk=256):
    L, D, N = w_in.shape
    return pl.pallas_call(
        _wprep_kernel,
        grid=(L, D // tk),
        in_specs=[pl.BlockSpec((None, tk, N), lambda l, i: (l, i, 0))],
        out_specs=pl.BlockSpec((None, tk, IN_PAD), lambda l, i: (l, i, 0)),
        out_shape=jax.ShapeDtypeStruct((L, D, IN_PAD), BF16),
        compiler_params=pltpu.CompilerParams(
            dimension_semantics=("arbitrary", "arbitrary"), vmem_limit_bytes=VMEM_LIMIT),
        name="wprep",
    )(w_in)


def _block_diag(w):
    nb, bw, _ = w.shape
    eye = jnp.eye(nb, dtype=w.dtype)
    return (eye[:, None, :, None] * w[:, :, None, :]).reshape(nb * bw, nb * bw)


def _layer(x2, mem2, prm, w_pad_all, layer, tiles, passes):
    (norm_g, w_in, conv_w, conv_b, lru_wa, lru_ba, lru_wx, lru_bx, lru_lambda,
     rw_mu, rw_w0, rw_w_up, rw_a0, rw_a_up, rw_k_k, rw_k_a, rw_r_k, rw_gn_g, rw_gn_b,
     swa_q_g, swa_k_g, swa_sinks, mem_norm_g, w_mem_kv, mem_q_g, mem_k_g, w_out) = prm
    row = lambda t: t.reshape(1, -1)

    p = _inproj(x2, row(norm_g), w_pad_all, layer, tiles["inproj"])

    mu_pad = jnp.concatenate([rw_mu, jnp.zeros((LANES - 2 * LORA,), F32)]).reshape(1, -1)
    zpad = jnp.zeros((LANES - LORA, GROUP_W), F32)
    wup_pad = jnp.concatenate([rw_w_up, zpad], axis=0)
    aup_pad = jnp.concatenate([jnp.zeros((LORA, GROUP_W), F32), rw_a_up,
                               jnp.zeros((LANES - 2 * LORA, GROUP_W), F32)], axis=0)
    o_rw = _rwkv(p, mu_pad, row(rw_w0), wup_pad, row(rw_a0), aup_pad, row(rw_k_k), row(rw_k_a),
                 row(rw_r_k), row(rw_gn_g), row(rw_gn_b), tiles["rwkv"], passes)

    kv = _memkv(mem2, row(mem_norm_g), w_mem_kv.astype(BF16), row(mem_k_g))
    two = lambda t: jnp.concatenate([t, t]).reshape(1, LANES)
    w_out_p = jnp.concatenate(
        [w_out[0:2 * GROUP_W], _perm_heads(w_out[2 * GROUP_W:3 * GROUP_W], 0), w_out[3 * GROUP_W:]],
        axis=0).astype(BF16)
    lru_prm = (conv_w, row(conv_b), _block_diag(lru_wa).astype(BF16), row(lru_ba),
               _block_diag(lru_wx).astype(BF16), row(lru_bx), row(lru_lambda))
    return _post(x2, p, o_rw, swa_sinks, lru_prm, (two(swa_q_g), two(swa_k_g)),
                 (kv, row(mem_q_g)), w_out_p, tiles["post"])


def _tiles(S):
    pick = lambda t: t if S % t == 0 else S
    return {"inproj": pick(256), "rwkv": pick(256), "post": pick(512)}


def kernel(x, mem, norm_g, w_in, conv_w, conv_b, lru_wa, lru_ba, lru_wx, lru_bx, lru_lambda, rw_mu, rw_w0, rw_w_up, rw_a0, rw_a_up, rw_k_k, rw_k_a, rw_r_k, rw_gn_g, rw_gn_b, swa_q_g, swa_k_g, swa_sinks, mem_norm_g, w_mem_kv, mem_q_g, mem_k_g, w_out):
    B, S, D = x.shape
    assert B == 1
    params = (norm_g, w_in, conv_w, conv_b, lru_wa, lru_ba, lru_wx, lru_bx, lru_lambda,
              rw_mu, rw_w0, rw_w_up, rw_a0, rw_a_up, rw_k_k, rw_k_a, rw_r_k, rw_gn_g, rw_gn_b,
              swa_q_g, swa_k_g, swa_sinks, mem_norm_g, w_mem_kv, mem_q_g, mem_k_g, w_out)
    x2 = x.reshape(S, D)
    mem2 = mem.reshape(mem.shape[1], D)
    tiles = _tiles(S)
    w_pad_all = _wprep(w_in)
    for l in range(norm_g.shape[0]):
        x2 = _layer(x2, mem2, tuple(t[l] for t in params), w_pad_all, l, tiles, passes=1)
    return x2.reshape(B, S, D)
```

```python
import functools

import jax
import jax.numpy as jnp
from jax import lax
from jax.experimental import pallas as pl
from jax.experimental.pallas import tpu as pltpu

F32 = jnp.float32
BF16 = jnp.bfloat16

LANES = 128
SUBLANES = 8
GROUP_W = 512
HEAD64 = 64
CHUNK = 64
CONV_W = 4
LRU_C = 8.0
LORA = 32
EPS = 1e-6
RWKV_GN_EPS = 64e-5
WINDOW = 128
N_MEM = 256
MEM_HEADS = 4
NEG_BIG = -1e30
VMEM_LIMIT = 56 * 1024 * 1024

COL_LRU_X, COL_LRU_G = 0, 512
COL_RW_R, COL_RW_K, COL_RW_V, COL_RW_G = 1024, 1536, 2048, 2560
COL_SWA_Q, COL_SWA_G = 3072, 3584
COL_MEM_Q, COL_MEM_G = 4096, 4608
COL_SWA_KV, COL_RW_LORA = 5120, 5376
IN_PAD = 5632
IN_HEAD = 2560

NN = (((1,), (0,)), ((), ()))
NT = (((1,), (1,)), ((), ()))
TN = (((0,), (0,)), ((), ()))
BNN = (((2,), (1,)), ((0,), (0,)))
BNT = (((2,), (2,)), ((0,), (0,)))
BTN = (((1,), (1,)), ((0,), (0,)))


def _dot(a, b, dims=NN):
    return lax.dot_general(a, b, dims, preferred_element_type=F32)


def _split(a):
    hi = a.astype(BF16)
    lo = (a - hi.astype(F32)).astype(BF16)
    return hi, lo


def _mm(a, b, dims=NN, passes=1):
    if passes == 1:
        return _dot(a.astype(BF16), b.astype(BF16), dims)
    ah, al = _split(a)
    bh, bl = _split(b)
    return _dot(ah, bh, dims) + (_dot(ah, bl, dims) + _dot(al, bh, dims))


def _mm_exact_rhs(a, b_bf16):
    ah, al = _split(a)
    return _dot(ah, b_bf16) + _dot(al, b_bf16)


def _iota2(shape, axis):
    return lax.broadcasted_iota(jnp.int32, shape, axis)


def _head_ones():
    r = _iota2((LANES, LANES), 0) // HEAD64
    c = _iota2((LANES, LANES), 1) // HEAD64
    return (r == c).astype(BF16)


def _segsum64(x, ones_bd):
    n, w = x.shape
    cols = [_mm_exact_rhs(x[:, j:j + LANES], ones_bd) for j in range(0, w, LANES)]
    return cols[0] if len(cols) == 1 else jnp.concatenate(cols, axis=1)


def _norm64(x, gain):
    ss = _segsum64(x * x, _head_ones())
    return x * lax.rsqrt(ss * (1.0 / HEAD64) + EPS) * gain


def _softplus(z):
    return jnp.maximum(z, 0.0) + jnp.log1p(jnp.exp(-jnp.abs(z)))


def _sigmoid(z):
    return 0.5 * jnp.tanh(0.5 * z) + 0.5


def _silu(z):
    return z * _sigmoid(z)


def _inproj_kernel(x_ref, g_ref, w_ref, o_ref):
    x = x_ref[...]
    ms = jnp.mean(x * x, axis=-1, keepdims=True)
    h = (x * lax.rsqrt(ms + EPS) * g_ref[...]).astype(BF16)
    o_ref[...] = _dot(h, w_ref[...])


def _inproj(x2, g, w_pad_all, layer, tm):
    S, D = x2.shape
    return pl.pallas_call(
        _inproj_kernel,
        grid=(S // tm,),
        in_specs=[
            pl.BlockSpec((tm, D), lambda i: (i, 0)),
            pl.BlockSpec((1, D), lambda i: (0, 0)),
            pl.BlockSpec((None, D, IN_PAD), lambda i: (layer, 0, 0),
                         pipeline_mode=pl.Buffered(1)),
        ],
        out_specs=pl.BlockSpec((tm, IN_PAD), lambda i: (i, 0)),
        out_shape=jax.ShapeDtypeStruct((S, IN_PAD), F32),
        compiler_params=pltpu.CompilerParams(
            dimension_semantics=("arbitrary",), vmem_limit_bytes=VMEM_LIMIT),
        name="inproj",
    )(x2, g, w_pad_all)


def _lru_block(j, x_ref, g_ref, cw_ref, cb_ref, wa_ref, ba_ref, wx_ref, bx_ref, lam_ref,
               xbuf_ref, h_ref):
    tm = x_ref.shape[0]
    c0, c1 = j * LANES, (j + 1) * LANES
    x = x_ref[:, c0:c1]
    xbuf_ref[8:8 + tm, c0:c1] = x
    conv = cb_ref[:, c0:c1] + cw_ref[CONV_W - 1:CONV_W, c0:c1] * x
    for t in range(CONV_W - 1):
        off = 8 - (CONV_W - 1) + t
        conv = conv + cw_ref[t:t + 1, c0:c1] * xbuf_ref[off:off + tm, c0:c1]
    xbuf_ref[0:8, c0:c1] = x[tm - 8:tm, :]

    cb16 = conv.astype(BF16)
    r = _sigmoid(_dot(cb16, wa_ref[c0:c1, c0:c1]) + ba_ref[:, c0:c1])
    ig = _sigmoid(_dot(cb16, wx_ref[c0:c1, c0:c1]) + bx_ref[:, c0:c1])
    log_a = (-LRU_C) * r * _softplus(-lam_ref[:, c0:c1])
    a = jnp.exp(log_a)
    m2 = jnp.maximum(-jnp.tanh(log_a) * (a * a + 1.0), 1e-12)
    u = (m2 * lax.rsqrt(m2)) * (ig * conv)

    n_slabs = tm // SUBLANES
    a3 = a.reshape(n_slabs, SUBLANES, LANES)
    u3 = u.reshape(n_slabs, SUBLANES, LANES)
    sub = lax.broadcasted_iota(jnp.int32, (1, SUBLANES, LANES), 1)
    d = 1
    while d < SUBLANES:
        keep = sub >= d
        a_sh = jnp.where(keep, pltpu.roll(a3, d, 1), 1.0)
        u_sh = jnp.where(keep, pltpu.roll(u3, d, 1), 0.0)
        u3 = u3 + a3 * u_sh
        a3 = a3 * a_sh
        d *= 2
    h = h_ref[:, c0:c1]
    slabs = []
    for s in range(n_slabs):
        h_s = u3[s] + a3[s] * h
        slabs.append(h_s)
        h = h_s[SUBLANES - 1:SUBLANES, :]
    h_ref[:, c0:c1] = h
    return jnp.concatenate(slabs, axis=0) * _silu(g_ref[:, c0:c1])


def _rwkv_kernel(r_ref, k_ref, v_ref, l_ref, g_ref, mu_ref, w0_ref, wup_ref, a0_ref, aup_ref,
                 kk_ref, ka_ref, rk_ref, gng_ref, gnb_ref, o_ref,
                 prev_ref, state_ref, *, passes):
    tm = r_ref.shape[0]
    n_chunks = tm // CHUNK
    n_pairs = GROUP_W // LANES
    WIN = 3 * GROUP_W + LANES

    @pl.when(pl.program_id(0) == 0)
    def _():
        prev_ref[...] = jnp.zeros_like(prev_ref)
        state_ref[...] = jnp.zeros_like(state_ref)

    row_t = _iota2((tm, 1), 0)

    def shift_lerp(ref, c0, c1):
        x = ref[...]
        xs = jnp.where(row_t == 0, prev_ref[:, c0:c1], pltpu.roll(x, 1, 0))
        prev_ref[:, c0:c1] = x[tm - 1:tm, :]
        return x + (xs - x) * mu_ref[:, c0:c1]

    r = shift_lerp(r_ref, 0, GROUP_W)
    k = shift_lerp(k_ref, GROUP_W, 2 * GROUP_W)
    v = shift_lerp(v_ref, 2 * GROUP_W, 3 * GROUP_W)
    lo_in = shift_lerp(l_ref, 3 * GROUP_W, WIN)

    lane_l = _iota2((tm, LANES), 1)
    lora = jnp.where(lane_l < LORA, jnp.tanh(lo_in), lo_in)
    w = -_softplus(-(w0_ref[...] + _mm(lora, wup_ref[...], passes=3))) - 0.5
    lw = -jnp.exp(w)
    a = _sigmoid(a0_ref[...] + _mm(lora, aup_ref[...], passes=3))

    ones_bd = _head_ones()
    kk = k * kk_ref[...]
    kk = kk / jnp.maximum(jnp.sqrt(_segsum64(kk * kk, ones_bd)), 1e-12)
    k2 = k * (1.0 + (a - 1.0) * ka_ref[...])
    b = kk * a
    bonus = _segsum64(r * k2 * rk_ref[...], ones_bd) * v

    cl = lw
    rows_c = _iota2((tm, GROUP_W), 0) % CHUNK
    d = 1
    while d < CHUNK:
        cl = cl + jnp.where(rows_c >= d, pltpu.roll(cl, d, 0), 0.0)
        d *= 2

    to3 = lambda t: t.reshape(n_chunks, CHUNK, GROUP_W)
    cl3 = to3(cl)
    cl_end = cl3[:, CHUNK - 1:CHUNK, :]
    e_inv = jnp.exp(-cl3)
    e_end = jnp.exp(cl_end - cl3)
    w_end = jnp.exp(cl_end)
    kk3, b3, k23 = to3(kk), to3(b), to3(k2)

    lo_lane = lax.broadcasted_iota(jnp.int32, (1, CHUNK, LANES), 2) < HEAD64
    n_units = n_chunks * n_pairs

    def stack(t3, dtype):
        units = []
        for p in range(n_pairs):
            x = t3[:, :, p * LANES:(p + 1) * LANES]
            units.append(jnp.concatenate(
                [jnp.where(lo_lane, x, 0.0), jnp.where(lo_lane, 0.0, x)], axis=1).astype(dtype))
        return jnp.stack(units, axis=1).reshape(n_units, 2 * CHUNK, LANES)

    op_dtype = BF16 if passes == 1 else F32
    cast = lambda t: t.astype(op_dtype)
    mm = functools.partial(_mm, passes=passes)

    a_s = stack(-kk3 * jnp.exp(cl3 - to3(lw)), op_dtype)
    r_s = stack(to3(r) * jnp.exp(cl3), F32)
    b_s = stack(b3 * e_inv, op_dtype)
    k_s = stack(k23 * e_inv, op_dtype)
    be_s = stack(b3 * e_end, op_dtype)
    ke_s = stack(k23 * e_end, op_dtype)
    v_s = stack(to3(v), op_dtype)
    wend_u = jnp.stack([w_end[:, :, p * LANES:(p + 1) * LANES] for p in range(n_pairs)],
                       axis=1).reshape(n_units, 1, LANES)

    ri = lax.broadcasted_iota(jnp.int32, (1, LANES, LANES), 1)
    ci = lax.broadcasted_iota(jnp.int32, (1, LANES, LANES), 2)
    same_head = (ri // CHUNK) == (ci // CHUNK)
    strict = same_head & (ci < ri)
    incl = same_head & (ci <= ri)
    diag16 = strict & ((ri // 16) == (ci // 16))
    off32 = strict & ((ri // 32) == (ci // 32)) & ((ri // 16) != (ci // 16))
    off64 = strict & ((ri // 32) != (ci // 32))
    eye = (ri == ci).astype(F32)

    g = mm(jnp.concatenate([a_s, cast(r_s)], axis=1), jnp.concatenate([b_s, k_s], axis=1), BNT)
    g_ab, g_ak = g[:, 0:LANES, 0:LANES], g[:, 0:LANES, LANES:2 * LANES]
    g_rb, g_rk = g[:, LANES:2 * LANES, 0:LANES], g[:, LANES:2 * LANES, LANES:2 * LANES]
    l_ak = cast(jnp.where(strict, g_ak, 0.0))
    m_rb = cast(jnp.where(incl, g_rb, 0.0))
    m_rk = cast(jnp.where(incl, g_rk, 0.0))

    d1 = jnp.where(diag16, g_ab, 0.0)
    d1c = cast(d1)
    d2 = cast(mm(d1c, d1c, BNN))
    d4 = cast(mm(d2, d2, BNN))
    d8 = mm(d4, d4, BNN)
    t = eye + d1
    t = t + mm(cast(t), d2, BNN)
    t = t + mm(cast(t), d4, BNN)
    t = t + mm(cast(t), d8, BNN)
    tc = cast(t)
    t = t + mm(tc, mm(cast(jnp.where(off32, g_ab, 0.0)), tc, BNN), BNN)
    tc = cast(t)
    t = t + mm(tc, mm(cast(jnp.where(off64, g_ab, 0.0)), tc, BNN), BNN)

    lv = mm(l_ak, v_s, BNN)
    tu = cast(mm(cast(t), jnp.concatenate([a_s, cast(lv)], axis=2), BNN))
    ru = mm(m_rb, tu, BNN)
    r_hat = cast(r_s + ru[:, :, 0:LANES])
    y_v = ru[:, :, LANES:2 * LANES] + mm(m_rk, v_s, BNN)
    pq = mm(be_s, tu, BTN)
    pp = cast(pq[:, :, 0:LANES])
    q = pq[:, :, LANES:2 * LANES] + mm(ke_s, v_s, BTN)
    wcol = jnp.sum(eye * wend_u, axis=2, keepdims=True)

    s0 = state_ref[...]
    y_parts = []
    for c in range(n_chunks):
        u0, u1 = c * n_pairs, (c + 1) * n_pairs
        s0c = cast(s0)
        y_parts.append(mm(r_hat[u0:u1], s0c, BNN) + y_v[u0:u1])
        s0 = wcol[u0:u1] * s0 + mm(pp[u0:u1], s0c, BNN) + q[u0:u1]
    state_ref[...] = s0
    y_s = jnp.concatenate(y_parts, axis=0)
    y4 = (y_s[:, 0:CHUNK, :] + y_s[:, CHUNK:2 * CHUNK, :]).reshape(n_chunks, n_pairs, CHUNK, LANES)
    y = jnp.concatenate([y4[:, p] for p in range(n_pairs)], axis=-1).reshape(tm, GROUP_W)
    inv_n = 1.0 / HEAD64
    mean = _segsum64(y, ones_bd) * inv_n
    yc = y - mean
    var = _segsum64(yc * yc, ones_bd) * inv_n
    yn = yc * lax.rsqrt(var + RWKV_GN_EPS) * gng_ref[...] + gnb_ref[...]
    o_ref[...] = (yn + bonus) * _silu(g_ref[...])


def _rwkv(p, mu, w0, wup, a0, aup, kkw, kaw, rkw, gng, gnb, tm, passes):
    S = p.shape[0]
    row = lambda n=1, w=GROUP_W: pl.BlockSpec((n, w), lambda i: (0, 0))
    col = lambda off, w=GROUP_W: pl.BlockSpec((tm, w), lambda i: (i, off // w))
    WIN = 3 * GROUP_W + LANES
    return pl.pallas_call(
        functools.partial(_rwkv_kernel, passes=passes),
        grid=(S // tm,),
        in_specs=[
            col(COL_RW_R), col(COL_RW_K), col(COL_RW_V), col(COL_RW_LORA, LANES), col(COL_RW_G),
            row(1, WIN), row(), row(LANES), row(), row(LANES),
            row(), row(), row(), row(), row(),
        ],
        out_specs=pl.BlockSpec((tm, GROUP_W), lambda i: (i, 0)),
        out_shape=jax.ShapeDtypeStruct((S, GROUP_W), F32),
        scratch_shapes=[
            pltpu.VMEM((1, WIN), F32),
            pltpu.VMEM((GROUP_W // LANES, LANES, LANES), F32),
        ],
        compiler_params=pltpu.CompilerParams(
            dimension_semantics=("arbitrary",), vmem_limit_bytes=VMEM_LIMIT),
        name="rwkv7",
    )(p, p, p, p, p, mu, w0, wup, a0, aup, kkw, kaw, rkw, gng, gnb)


def _swa_prepare(kv_ref, kvp_ref, kg_ref):
    tq = kv_ref.shape[0]
    nblk = tq // WINDOW
    kv_all = jnp.concatenate([kvp_ref[...], kv_ref[...]], axis=0)
    k_all = _norm64(kv_all[:, 0:LANES], kg_ref[...]).astype(BF16)
    v_all = kv_all[:, LANES:2 * LANES]
    lo_v = _iota2((WINDOW + tq, LANES), 1) < HEAD64
    v_lo_all = jnp.where(lo_v, v_all, 0.0).astype(BF16)
    v_hi_all = jnp.where(lo_v, 0.0, v_all).astype(BF16)

    window = lambda t: jnp.stack(
        [t[b * WINDOW:(b + 2) * WINDOW, :] for b in range(nblk)], axis=0)

    shape3 = (nblk, 2 * WINDOW, 2 * WINDOW)
    qi = lax.broadcasted_iota(jnp.int32, shape3, 1) % WINDOW
    kj = lax.broadcasted_iota(jnp.int32, shape3, 2)
    blk = lax.broadcasted_iota(jnp.int32, shape3, 0)
    has_prev = (kj >= WINDOW) | (blk > 0) | (pl.program_id(0) > 0)
    mask = (kj > qi) & (kj <= qi + WINDOW) & has_prev
    return window(k_all), window(v_lo_all), window(v_hi_all), mask


def _swa_block(i, prepared, sink_ref, q_ref, g_ref, qg_ref):
    keys, v_lo, v_hi, mask = prepared
    tq = q_ref.shape[0]
    nblk = tq // WINDOW
    c0, c1 = i * LANES, (i + 1) * LANES
    upper_rows = lax.broadcasted_iota(jnp.int32, (1, 2 * WINDOW, 1), 1) >= WINDOW
    lo_q = lax.broadcasted_iota(jnp.int32, (1, WINDOW, LANES), 2) < HEAD64
    qn = _norm64(q_ref[:, c0:c1], qg_ref[...] * (HEAD64 ** -0.5)).reshape(nblk, WINDOW, LANES)
    q_s = jnp.concatenate([jnp.where(lo_q, qn, 0.0), jnp.where(lo_q, 0.0, qn)],
                          axis=1).astype(BF16)
    s = jnp.where(mask, _dot(q_s, keys, BNT), NEG_BIG)
    sink = jnp.where(upper_rows, sink_ref[4 + i], sink_ref[i])
    m = jnp.maximum(jnp.max(s, axis=-1, keepdims=True), sink)
    e = jnp.exp(s - m)
    rden = 1.0 / (jnp.sum(e, axis=-1, keepdims=True) + jnp.exp(sink - m))
    e = e.astype(BF16)
    out = (_dot(e[:, 0:WINDOW, :], v_lo, BNN) * rden[:, 0:WINDOW, :]
           + _dot(e[:, WINDOW:2 * WINDOW, :], v_hi, BNN) * rden[:, WINDOW:2 * WINDOW, :])
    return out.reshape(tq, LANES) * _silu(g_ref[:, c0:c1])


def _memkv_kernel(m_ref, g_ref, w_ref, kg_ref, o_ref, h_ref):
    j = pl.program_id(0)

    @pl.when(j == 0)
    def _():
        x = m_ref[...]
        ms = jnp.mean(x * x, axis=-1, keepdims=True)
        h_ref[...] = (x * lax.rsqrt(ms + EPS) * g_ref[...]).astype(BF16)

    kv = _dot(h_ref[...], w_ref[...])

    @pl.when(j < MEM_HEADS)
    def _():
        ms = jnp.mean(kv * kv, axis=-1, keepdims=True)
        o_ref[...] = kv * lax.rsqrt(ms + EPS) * kg_ref[...]

    @pl.when(j >= MEM_HEADS)
    def _():
        o_ref[...] = kv


def _memkv(mem2, g, w_kv, kg):
    M, D = mem2.shape
    N = w_kv.shape[1]
    return pl.pallas_call(
        _memkv_kernel,
        grid=(N // LANES,),
        in_specs=[
            pl.BlockSpec((M, D), lambda j: (0, 0)),
            pl.BlockSpec((1, D), lambda j: (0, 0)),
            pl.BlockSpec((D, LANES), lambda j: (0, j)),
            pl.BlockSpec((1, LANES), lambda j: (0, 0)),
        ],
        out_specs=pl.BlockSpec((M, LANES), lambda j: (0, j)),
        out_shape=jax.ShapeDtypeStruct((M, N), F32),
        scratch_shapes=[pltpu.VMEM((M, D), BF16)],
        compiler_params=pltpu.CompilerParams(
            dimension_semantics=("arbitrary",), vmem_limit_bytes=VMEM_LIMIT),
        name="memkv",
    )(mem2, g, w_kv, kg)


def _mem_head(h, q_ref, g_ref, kv_ref, qg_ref):
    head = GROUP_W // MEM_HEADS
    c0, c1 = h * head, (h + 1) * head
    q = q_ref[:, c0:c1]
    gain = qg_ref[...] * (head ** -0.5)
    qn = q * lax.rsqrt(jnp.mean(q * q, axis=-1, keepdims=True) + EPS) * gain
    s = _mm(qn, kv_ref[:, c0:c1], NT)
    m = jnp.max(s, axis=-1, keepdims=True)
    e = jnp.exp(s - m)
    rden = 1.0 / jnp.sum(e, axis=-1, keepdims=True)
    out = _mm(e, kv_ref[:, GROUP_W + c0:GROUP_W + c1]) * rden
    return out * _silu(g_ref[:, c0:c1])


def _post_kernel(sink_ref, x_ref, orw_ref, lx_ref, lg_ref, sq_ref, skv_ref, skvp_ref, sg_ref,
                 mq_ref, mg_ref, cw_ref, cb_ref, wa_ref, ba_ref, wx_ref, bx_ref, lam_ref,
                 sqg_ref, skg_ref, mkv_ref, mqg_ref, w_ref, y_ref, xbuf_ref, h_ref):
    @pl.when(pl.program_id(0) == 0)
    def _():
        xbuf_ref[0:8, :] = jnp.zeros((8, GROUP_W), F32)
        h_ref[...] = jnp.zeros_like(h_ref)

    n_stage = GROUP_W // LANES
    piece = y_ref.shape[1] // n_stage

    def project(o16, group, j):
        return _dot(o16, w_ref[group * GROUP_W:(group + 1) * GROUP_W, j * piece:(j + 1) * piece])

    def accumulate(stage_fn, o16, group, first=False):
        cols = []
        for j in range(n_stage):
            cols.append(stage_fn(j))
            n0, n1 = j * piece, (j + 1) * piece
            base = x_ref[:, n0:n1] if first else y_ref[:, n0:n1]
            y_ref[:, n0:n1] = base + project(o16, group, j)
        return jnp.concatenate(cols, axis=1).astype(BF16)

    lru = functools.partial(_lru_block, x_ref=lx_ref, g_ref=lg_ref, cw_ref=cw_ref, cb_ref=cb_ref,
                            wa_ref=wa_ref, ba_ref=ba_ref, wx_ref=wx_ref, bx_ref=bx_ref,
                            lam_ref=lam_ref, xbuf_ref=xbuf_ref, h_ref=h_ref)
    o_lru = accumulate(lru, orw_ref[...].astype(BF16), 1, first=True)
    prepared = _swa_prepare(skv_ref, skvp_ref, skg_ref)
    swa = functools.partial(_swa_block, prepared=prepared, sink_ref=sink_ref, q_ref=sq_ref,
                            g_ref=sg_ref, qg_ref=sqg_ref)
    o_swa = accumulate(swa, o_lru, 0)
    mem = functools.partial(_mem_head, q_ref=mq_ref, g_ref=mg_ref, kv_ref=mkv_ref, qg_ref=mqg_ref)
    o_mem = accumulate(mem, o_swa, 2)
    for j in range(n_stage):
        n0, n1 = j * piece, (j + 1) * piece
        y_ref[:, n0:n1] = y_ref[:, n0:n1] + project(o_mem, 3, j)


def _post(x2, p, o_rw, sinks, lru_prm, swa_prm, mem_prm, w_out, tm):
    S, D = x2.shape
    nblk = tm // WINDOW
    col = lambda off, w=GROUP_W: pl.BlockSpec((tm, w), lambda i: (i, off // w))
    const = lambda shape: pl.BlockSpec(shape, lambda i: (0,) * len(shape))
    cw, cb, wa_bd, ba, wx_bd, bx, lam = lru_prm
    qg2, kg2 = swa_prm
    kv, mqg = mem_prm
    return pl.pallas_call(
        _post_kernel,
        grid=(S // tm,),
        in_specs=[
            pl.BlockSpec(memory_space=pltpu.SMEM),
            pl.BlockSpec((tm, D), lambda i: (i, 0)),
            pl.BlockSpec((tm, GROUP_W), lambda i: (i, 0)),
            col(COL_LRU_X), col(COL_LRU_G),
            col(COL_SWA_Q), col(COL_SWA_KV, 2 * LANES),
            pl.BlockSpec((WINDOW, 2 * LANES),
                         lambda i: (jnp.maximum(i * nblk - 1, 0), COL_SWA_KV // (2 * LANES))),
            col(COL_SWA_G), col(COL_MEM_Q), col(COL_MEM_G),
            const((CONV_W, GROUP_W)), const((1, GROUP_W)),
            const((GROUP_W, GROUP_W)), const((1, GROUP_W)),
            const((GROUP_W, GROUP_W)), const((1, GROUP_W)), const((1, GROUP_W)),
            const((1, LANES)), const((1, LANES)),
            const((N_MEM, 2 * GROUP_W)), const((1, GROUP_W // MEM_HEADS)),
            pl.BlockSpec((4 * GROUP_W, D), lambda i: (0, 0), pipeline_mode=pl.Buffered(1)),
        ],
        out_specs=pl.BlockSpec((tm, D), lambda i: (i, 0)),
        out_shape=jax.ShapeDtypeStruct((S, D), F32),
        scratch_shapes=[pltpu.VMEM((tm + 8, GROUP_W), F32), pltpu.VMEM((1, GROUP_W), F32)],
        compiler_params=pltpu.CompilerParams(
            dimension_semantics=("arbitrary",), vmem_limit_bytes=VMEM_LIMIT),
        name="post",
    )(sinks, x2, o_rw, p, p, p, p, p, p, p, p, cw, cb, wa_bd, ba, wx_bd, bx, lam,
      qg2, kg2, kv, mqg, w_out)


SWA_PERM = (0, 4, 1, 5, 2, 6, 3, 7)


def _perm_heads(t, axis):
    parts = jnp.split(t, 8, axis=axis)
    return jnp.concatenate([parts[h] for h in SWA_PERM], axis=axis)


def _wprep_kernel(wt_ref, o_ref):
    tk = wt_ref.shape[1]

    def put(dst, src_rows):
        col = dst
        for start, count in src_rows:
            for r in range(0, count, LANES):
                n = min(LANES, count - r)
                o_ref[:, col:col + n] = wt_ref[start + r:start + r + n, :].T.astype(BF16)
                col += n

    put(0, [(0, IN_HEAD)])
    src_g_rw = IN_HEAD + 2 * LORA
    src_q = src_g_rw + GROUP_W
    src_kv = src_q + GROUP_W
    src_g = src_kv + 2 * LANES
    put(COL_RW_G, [(src_g_rw, GROUP_W)])
    for base, dst in ((src_q, COL_SWA_Q), (src_g, COL_SWA_G)):
        for i in range(GROUP_W // LANES):
            put(dst + i * LANES, [(base + HEAD64 * i, HEAD64), (base + HEAD64 * (4 + i), HEAD64)])
    put(COL_MEM_Q, [(src_g + GROUP_W, 2 * GROUP_W)])
    put(COL_SWA_KV, [(src_kv, 2 * LANES)])
    put(COL_RW_LORA, [(IN_HEAD, 2 * LORA)])
    pad0 = COL_RW_LORA + 2 * LORA
    o_ref[:, pad0:IN_PAD] = jnp.zeros((tk, IN_PAD - pad0), BF16)


def _wprep(w_in, tk=256):
    L, D, N = w_in.shape
    return pl.pallas_call(
        _wprep_kernel,
        grid=(L, D // tk),
        in_specs=[pl.BlockSpec((None, N, tk), lambda l, i: (l, 0, i))],
        out_specs=pl.BlockSpec((None, tk, IN_PAD), lambda l, i: (l, i, 0)),
        out_shape=jax.ShapeDtypeStruct((L, D, IN_PAD), BF16),
        compiler_params=pltpu.CompilerParams(
            dimension_semantics=("arbitrary", "arbitrary"), vmem_limit_bytes=VMEM_LIMIT),
        name="wprep",
    )(jnp.swapaxes(w_in, 1, 2))


def _block_diag(w):
    nb, bw, _ = w.shape
    eye = jnp.eye(nb, dtype=w.dtype)
    return (eye[:, None, :, None] * w[:, :, None, :]).reshape(nb * bw, nb * bw)


def _layer(x2, mem2, prm, w_pad_all, layer, tiles, passes):
    (norm_g, w_in, conv_w, conv_b, lru_wa, lru_ba, lru_wx, lru_bx, lru_lambda,
     rw_mu, rw_w0, rw_w_up, rw_a0, rw_a_up, rw_k_k, rw_k_a, rw_r_k, rw_gn_g, rw_gn_b,
     swa_q_g, swa_k_g, swa_sinks, mem_norm_g, w_mem_kv, mem_q_g, mem_k_g, w_out) = prm
    row = lambda t: t.reshape(1, -1)

    p = _inproj(x2, row(norm_g), w_pad_all, layer, tiles["inproj"])

    mu_pad = jnp.concatenate([rw_mu, jnp.zeros((LANES - 2 * LORA,), F32)]).reshape(1, -1)
    zpad = jnp.zeros((LANES - LORA, GROUP_W), F32)
    wup_pad = jnp.concatenate([rw_w_up, zpad], axis=0)
    aup_pad = jnp.concatenate([jnp.zeros((LORA, GROUP_W), F32), rw_a_up,
                               jnp.zeros((LANES - 2 * LORA, GROUP_W), F32)], axis=0)
    o_rw = _rwkv(p, mu_pad, row(rw_w0), wup_pad, row(rw_a0), aup_pad, row(rw_k_k), row(rw_k_a),
                 row(rw_r_k), row(rw_gn_g), row(rw_gn_b), tiles["rwkv"], passes)

    kv = _memkv(mem2, row(mem_norm_g), w_mem_kv.astype(BF16), row(mem_k_g))
    two = lambda t: jnp.concatenate([t, t]).reshape(1, LANES)
    w_out_p = jnp.concatenate(
        [w_out[0:2 * GROUP_W], _perm_heads(w_out[2 * GROUP_W:3 * GROUP_W], 0), w_out[3 * GROUP_W:]],
        axis=0).astype(BF16)
    lru_prm = (conv_w, row(conv_b), _block_diag(lru_wa).astype(BF16), row(lru_ba),
               _block_diag(lru_wx).astype(BF16), row(lru_bx), row(lru_lambda))
    return _post(x2, p, o_rw, swa_sinks, lru_prm, (two(swa_q_g), two(swa_k_g)),
                 (kv, row(mem_q_g)), w_out_p, tiles["post"])


def _tiles(S):
    pick = lambda t: t if S % t == 0 else S
    return {"inproj": pick(256), "rwkv": pick(256), "post": pick(512)}


def kernel(x, mem, norm_g, w_in, conv_w, conv_b, lru_wa, lru_ba, lru_wx, lru_bx, lru_lambda, rw_mu, rw_w0, rw_w_up, rw_a0, rw_a_up, rw_k_k, rw_k_a, rw_r_k, rw_gn_g, rw_gn_b, swa_q_g, swa_k_g, swa_sinks, mem_norm_g, w_mem_kv, mem_q_g, mem_k_g, w_out):
    B, S, D = x.shape
    assert B == 1
    params = (norm_g, w_in, conv_w, conv_b, lru_wa, lru_ba, lru_wx, lru_bx, lru_lambda,
              rw_mu, rw_w0, rw_w_up, rw_a0, rw_a_up, rw_k_k, rw_k_a, rw_r_k, rw_gn_g, rw_gn_b,
              swa_q_g, swa_k_g, swa_sinks, mem_norm_g, w_mem_kv, mem_q_g, mem_k_g, w_out)
    x2 = x.reshape(S, D)
    mem2 = mem.reshape(mem.shape[1], D)
    tiles = _tiles(S)
    w_pad_all = _wprep(w_in)
    for l in range(norm_g.shape[0]):
        x2 = _layer(x2, mem2, tuple(t[l] for t in params), w_pad_all, l, tiles, passes=1)
    return x2.reshape(B, S, D)
```

```python
import functools

import jax
import jax.numpy as jnp
from jax import lax
from jax.experimental import pallas as pl
from jax.experimental.pallas import tpu as pltpu

F32 = jnp.float32
BF16 = jnp.bfloat16

LANES = 128
SUBLANES = 8
GROUP_W = 512
HEAD64 = 64
CHUNK = 64
CONV_W = 4
LRU_C = 8.0
LORA = 32
EPS = 1e-6
RWKV_GN_EPS = 64e-5
WINDOW = 128
N_MEM = 256
MEM_HEADS = 4
NEG_BIG = -1e30
VMEM_LIMIT = 56 * 1024 * 1024

COL_LRU_X, COL_LRU_G = 0, 512
COL_RW_R, COL_RW_K, COL_RW_V, COL_RW_G = 1024, 1536, 2048, 2560
COL_SWA_Q, COL_SWA_G = 3072, 3584
COL_MEM_Q, COL_MEM_G = 4096, 4608
COL_SWA_KV, COL_RW_LORA = 5120, 5376
IN_PAD = 5632
IN_HEAD = 2560

NN = (((1,), (0,)), ((), ()))
NT = (((1,), (1,)), ((), ()))
TN = (((0,), (0,)), ((), ()))
BNN = (((2,), (1,)), ((0,), (0,)))
BNT = (((2,), (2,)), ((0,), (0,)))
BTN = (((1,), (1,)), ((0,), (0,)))


def _dot(a, b, dims=NN):
    return lax.dot_general(a, b, dims, preferred_element_type=F32)


def _split(a):
    hi = a.astype(BF16)
    lo = (a - hi.astype(F32)).astype(BF16)
    return hi, lo


def _mm(a, b, dims=NN, passes=1):
    if passes == 1:
        return _dot(a.astype(BF16), b.astype(BF16), dims)
    ah, al = _split(a)
    bh, bl = _split(b)
    return _dot(ah, bh, dims) + (_dot(ah, bl, dims) + _dot(al, bh, dims))


def _mm_exact_rhs(a, b_bf16):
    ah, al = _split(a)
    return _dot(ah, b_bf16) + _dot(al, b_bf16)


def _iota2(shape, axis):
    return lax.broadcasted_iota(jnp.int32, shape, axis)


def _head_ones():
    r = _iota2((LANES, LANES), 0) // HEAD64
    c = _iota2((LANES, LANES), 1) // HEAD64
    return (r == c).astype(BF16)


def _segsum64(x, ones_bd):
    n, w = x.shape
    cols = [_mm_exact_rhs(x[:, j:j + LANES], ones_bd) for j in range(0, w, LANES)]
    return cols[0] if len(cols) == 1 else jnp.concatenate(cols, axis=1)


def _norm64(x, gain):
    ss = _segsum64(x * x, _head_ones())
    return x * lax.rsqrt(ss * (1.0 / HEAD64) + EPS) * gain


def _softplus(z):
    return jnp.maximum(z, 0.0) + jnp.log1p(jnp.exp(-jnp.abs(z)))


def _sigmoid(z):
    return 0.5 * jnp.tanh(0.5 * z) + 0.5


def _silu(z):
    return z * _sigmoid(z)


def _inproj_kernel(x_ref, g_ref, w_ref, o_ref):
    x = x_ref[...]
    ms = jnp.mean(x * x, axis=-1, keepdims=True)
    h = (x * lax.rsqrt(ms + EPS) * g_ref[...]).astype(BF16)
    o_ref[...] = _dot(h, w_ref[...])


def _inproj(x2, g, w_pad_all, layer, tm):
    S, D = x2.shape
    return pl.pallas_call(
        _inproj_kernel,
        grid=(S // tm,),
        in_specs=[
            pl.BlockSpec((tm, D), lambda i: (i, 0)),
            pl.BlockSpec((1, D), lambda i: (0, 0)),
            pl.BlockSpec((None, D, IN_PAD), lambda i: (layer, 0, 0),
                         pipeline_mode=pl.Buffered(1)),
        ],
        out_specs=pl.BlockSpec((tm, IN_PAD), lambda i: (i, 0)),
        out_shape=jax.ShapeDtypeStruct((S, IN_PAD), F32),
        compiler_params=pltpu.CompilerParams(
            dimension_semantics=("arbitrary",), vmem_limit_bytes=VMEM_LIMIT),
        name="inproj",
    )(x2, g, w_pad_all)


def _lru_block(j, x_ref, g_ref, cw_ref, cb_ref, wa_ref, ba_ref, wx_ref, bx_ref, lam_ref,
               xbuf_ref, h_ref):
    tm = x_ref.shape[0]
    c0, c1 = j * LANES, (j + 1) * LANES
    x = x_ref[:, c0:c1]
    xbuf_ref[8:8 + tm, c0:c1] = x
    conv = cb_ref[:, c0:c1] + cw_ref[CONV_W - 1:CONV_W, c0:c1] * x
    for t in range(CONV_W - 1):
        off = 8 - (CONV_W - 1) + t
        conv = conv + cw_ref[t:t + 1, c0:c1] * xbuf_ref[off:off + tm, c0:c1]
    xbuf_ref[0:8, c0:c1] = x[tm - 8:tm, :]

    cb16 = conv.astype(BF16)
    r = _sigmoid(_dot(cb16, wa_ref[c0:c1, c0:c1]) + ba_ref[:, c0:c1])
    ig = _sigmoid(_dot(cb16, wx_ref[c0:c1, c0:c1]) + bx_ref[:, c0:c1])
    log_a = (-LRU_C) * r * _softplus(-lam_ref[:, c0:c1])
    a = jnp.exp(log_a)
    m2 = jnp.maximum(-jnp.tanh(log_a) * (a * a + 1.0), 1e-12)
    u = (m2 * lax.rsqrt(m2)) * (ig * conv)

    n_slabs = tm // SUBLANES
    a3 = a.reshape(n_slabs, SUBLANES, LANES)
    u3 = u.reshape(n_slabs, SUBLANES, LANES)
    sub = lax.broadcasted_iota(jnp.int32, (1, SUBLANES, LANES), 1)
    d = 1
    while d < SUBLANES:
        keep = sub >= d
        a_sh = jnp.where(keep, pltpu.roll(a3, d, 1), 1.0)
        u_sh = jnp.where(keep, pltpu.roll(u3, d, 1), 0.0)
        u3 = u3 + a3 * u_sh
        a3 = a3 * a_sh
        d *= 2
    h = h_ref[:, c0:c1]
    slabs = []
    for s in range(n_slabs):
        h_s = u3[s] + a3[s] * h
        slabs.append(h_s)
        h = h_s[SUBLANES - 1:SUBLANES, :]
    h_ref[:, c0:c1] = h
    return jnp.concatenate(slabs, axis=0) * _silu(g_ref[:, c0:c1])


def _rwkv_kernel(r_ref, k_ref, v_ref, l_ref, g_ref, mu_ref, w0_ref, wup_ref, a0_ref, aup_ref,
                 kk_ref, ka_ref, rk_ref, gng_ref, gnb_ref, o_ref,
                 prev_ref, state_ref, *, passes):
    tm = r_ref.shape[0]
    n_chunks = tm // CHUNK
    n_pairs = GROUP_W // LANES
    WIN = 3 * GROUP_W + LANES

    @pl.when(pl.program_id(0) == 0)
    def _():
        prev_ref[...] = jnp.zeros_like(prev_ref)
        state_ref[...] = jnp.zeros_like(state_ref)

    row_t = _iota2((tm, 1), 0)

    def shift_lerp(ref, c0, c1):
        x = ref[...]
        xs = jnp.where(row_t == 0, prev_ref[:, c0:c1], pltpu.roll(x, 1, 0))
        prev_ref[:, c0:c1] = x[tm - 1:tm, :]
        return x + (xs - x) * mu_ref[:, c0:c1]

    r = shift_lerp(r_ref, 0, GROUP_W)
    k = shift_lerp(k_ref, GROUP_W, 2 * GROUP_W)
    v = shift_lerp(v_ref, 2 * GROUP_W, 3 * GROUP_W)
    lo_in = shift_lerp(l_ref, 3 * GROUP_W, WIN)

    lane_l = _iota2((tm, LANES), 1)
    lora = jnp.where(lane_l < LORA, jnp.tanh(lo_in), lo_in)
    w = -_softplus(-(w0_ref[...] + _mm(lora, wup_ref[...], passes=3))) - 0.5
    lw = -jnp.exp(w)
    a = _sigmoid(a0_ref[...] + _mm(lora, aup_ref[...], passes=3))

    ones_bd = _head_ones()
    kk = k * kk_ref[...]
    kk = kk / jnp.maximum(jnp.sqrt(_segsum64(kk * kk, ones_bd)), 1e-12)
    k2 = k * (1.0 + (a - 1.0) * ka_ref[...])
    b = kk * a
    bonus = _segsum64(r * k2 * rk_ref[...], ones_bd) * v

    cl = lw
    rows_c = _iota2((tm, GROUP_W), 0) % CHUNK
    d = 1
    while d < CHUNK:
        cl = cl + jnp.where(rows_c >= d, pltpu.roll(cl, d, 0), 0.0)
        d *= 2

    to3 = lambda t: t.reshape(n_chunks, CHUNK, GROUP_W)
    cl3 = to3(cl)
    cl_end = cl3[:, CHUNK - 1:CHUNK, :]
    e_inv = jnp.exp(-cl3)
    e_end = jnp.exp(cl_end - cl3)
    w_end = jnp.exp(cl_end)
    kk3, b3, k23 = to3(kk), to3(b), to3(k2)

    lo_lane = lax.broadcasted_iota(jnp.int32, (1, CHUNK, LANES), 2) < HEAD64
    n_units = n_chunks * n_pairs
    op_dtype = BF16 if passes == 1 else F32
    cast = lambda t: t.astype(op_dtype)
    mm = functools.partial(_mm, passes=passes)

    def natural(t3):
        units = [t3[:, :, p * LANES:(p + 1) * LANES] for p in range(n_pairs)]
        return jnp.stack(units, axis=1).reshape(n_units, t3.shape[1], LANES)

    def stacked(x):
        x = cast(x)
        zero = jnp.zeros_like(x)
        return jnp.concatenate([jnp.where(lo_lane, x, zero), jnp.where(lo_lane, zero, x)], axis=1)

    a_n = natural(-kk3 * jnp.exp(cl3 - to3(lw)))
    r_n = natural(to3(r) * jnp.exp(cl3))
    a_s = stacked(a_n)
    b_s = stacked(natural(b3 * e_inv))
    k_s = stacked(natural(k23 * e_inv))
    be_s = stacked(natural(b3 * e_end))
    ke_s = stacked(natural(k23 * e_end))
    v_s = stacked(natural(to3(v)))
    wend_u = natural(w_end)

    ti = lax.broadcasted_iota(jnp.int32, (1, CHUNK, LANES), 1)
    si = lax.broadcasted_iota(jnp.int32, (1, CHUNK, LANES), 2) % CHUNK
    strict = si < ti
    incl = si <= ti
    diag16 = strict & ((ti // 16) == (si // 16))
    off32 = strict & ((ti // 32) == (si // 32)) & ((ti // 16) != (si // 16))
    off64 = strict & ((ti // 32) != (si // 32))
    eye = (si == ti).astype(F32)

    g = mm(cast(jnp.concatenate([a_n, r_n], axis=1)), jnp.concatenate([b_s, k_s], axis=1), BNT)
    g_ab, g_ak = g[:, 0:CHUNK, 0:LANES], g[:, 0:CHUNK, LANES:2 * LANES]
    g_rb, g_rk = g[:, CHUNK:2 * CHUNK, 0:LANES], g[:, CHUNK:2 * CHUNK, LANES:2 * LANES]
    l_ak = cast(jnp.where(strict, g_ak, 0.0))
    m_rb = cast(jnp.where(incl, g_rb, 0.0))
    m_rk = cast(jnp.where(incl, g_rk, 0.0))

    d1 = jnp.where(diag16, g_ab, 0.0)
    d1c = cast(d1)
    d2 = cast(mm(d1c, stacked(d1c), BNN))
    d2_s = stacked(d2)
    d4 = cast(mm(d2, d2_s, BNN))
    d4_s = stacked(d4)
    d8_s = stacked(mm(d4, d4_s, BNN))
    t = eye + d1
    t = t + mm(cast(t), d2_s, BNN)
    t = t + mm(cast(t), d4_s, BNN)
    t = t + mm(cast(t), d8_s, BNN)
    for off in (off32, off64):
        tc = cast(t)
        et = mm(cast(jnp.where(off, g_ab, 0.0)), stacked(tc), BNN)
        t = t + mm(tc, stacked(et), BNN)

    lv = mm(l_ak, v_s, BNN)
    tu = mm(cast(t), jnp.concatenate([a_s, stacked(lv)], axis=2), BNN)
    tu_s = jnp.concatenate([stacked(tu[:, :, 0:LANES]), stacked(tu[:, :, LANES:2 * LANES])], axis=2)
    ru = mm(m_rb, tu_s, BNN)
    r_hat = cast(r_n + ru[:, :, 0:LANES])
    y_v = ru[:, :, LANES:2 * LANES] + mm(m_rk, v_s, BNN)
    pq = mm(be_s, tu_s, BTN)
    pp = cast(pq[:, :, 0:LANES])
    q = pq[:, :, LANES:2 * LANES] + mm(ke_s, v_s, BTN)
    ri = lax.broadcasted_iota(jnp.int32, (1, LANES, LANES), 1)
    ci = lax.broadcasted_iota(jnp.int32, (1, LANES, LANES), 2)
    wcol = jnp.sum(jnp.where(ri == ci, wend_u, 0.0), axis=2, keepdims=True)

    s0 = state_ref[...]
    y_parts = []
    for c in range(n_chunks):
        u0, u1 = c * n_pairs, (c + 1) * n_pairs
        s0c = cast(s0)
        y_parts.append(mm(r_hat[u0:u1], s0c, BNN) + y_v[u0:u1])
        s0 = wcol[u0:u1] * s0 + mm(pp[u0:u1], s0c, BNN) + q[u0:u1]
    state_ref[...] = s0
    y4 = jnp.concatenate(y_parts, axis=0).reshape(n_chunks, n_pairs, CHUNK, LANES)
    y = jnp.concatenate([y4[:, p] for p in range(n_pairs)], axis=-1).reshape(tm, GROUP_W)
    inv_n = 1.0 / HEAD64
    mean = _segsum64(y, ones_bd) * inv_n
    yc = y - mean
    var = _segsum64(yc * yc, ones_bd) * inv_n
    yn = yc * lax.rsqrt(var + RWKV_GN_EPS) * gng_ref[...] + gnb_ref[...]
    o_ref[...] = (yn + bonus) * _silu(g_ref[...])


def _rwkv(p, mu, w0, wup, a0, aup, kkw, kaw, rkw, gng, gnb, tm, passes):
    S = p.shape[0]
    row = lambda n=1, w=GROUP_W: pl.BlockSpec((n, w), lambda i: (0, 0))
    col = lambda off, w=GROUP_W: pl.BlockSpec((tm, w), lambda i: (i, off // w))
    WIN = 3 * GROUP_W + LANES
    return pl.pallas_call(
        functools.partial(_rwkv_kernel, passes=passes),
        grid=(S // tm,),
        in_specs=[
            col(COL_RW_R), col(COL_RW_K), col(COL_RW_V), col(COL_RW_LORA, LANES), col(COL_RW_G),
            row(1, WIN), row(), row(LANES), row(), row(LANES),
            row(), row(), row(), row(), row(),
        ],
        out_specs=pl.BlockSpec((tm, GROUP_W), lambda i: (i, 0)),
        out_shape=jax.ShapeDtypeStruct((S, GROUP_W), F32),
        scratch_shapes=[
            pltpu.VMEM((1, WIN), F32),
            pltpu.VMEM((GROUP_W // LANES, LANES, LANES), F32),
        ],
        compiler_params=pltpu.CompilerParams(
            dimension_semantics=("arbitrary",), vmem_limit_bytes=VMEM_LIMIT),
        name="rwkv7",
    )(p, p, p, p, p, mu, w0, wup, a0, aup, kkw, kaw, rkw, gng, gnb)


def _swa_prepare(kv_ref, kvp_ref, kg_ref):
    tq = kv_ref.shape[0]
    nblk = tq // WINDOW
    kv_all = jnp.concatenate([kvp_ref[...], kv_ref[...]], axis=0)
    k_all = _norm64(kv_all[:, 0:LANES], kg_ref[...]).astype(BF16)
    v_all = kv_all[:, LANES:2 * LANES]
    lo_v = _iota2((WINDOW + tq, LANES), 1) < HEAD64
    v_lo_all = jnp.where(lo_v, v_all, 0.0).astype(BF16)
    v_hi_all = jnp.where(lo_v, 0.0, v_all).astype(BF16)

    window = lambda t: jnp.stack(
        [t[b * WINDOW:(b + 2) * WINDOW, :] for b in range(nblk)], axis=0)

    shape3 = (nblk, 2 * WINDOW, 2 * WINDOW)
    qi = lax.broadcasted_iota(jnp.int32, shape3, 1) % WINDOW
    kj = lax.broadcasted_iota(jnp.int32, shape3, 2)
    blk = lax.broadcasted_iota(jnp.int32, shape3, 0)
    has_prev = (kj >= WINDOW) | (blk > 0) | (pl.program_id(0) > 0)
    mask = (kj > qi) & (kj <= qi + WINDOW) & has_prev
    return window(k_all), window(v_lo_all), window(v_hi_all), mask


def _swa_block(i, prepared, sink_ref, q_ref, g_ref, qg_ref):
    keys, v_lo, v_hi, mask = prepared
    tq = q_ref.shape[0]
    nblk = tq // WINDOW
    c0, c1 = i * LANES, (i + 1) * LANES
    upper_rows = lax.broadcasted_iota(jnp.int32, (1, 2 * WINDOW, 1), 1) >= WINDOW
    lo_q = lax.broadcasted_iota(jnp.int32, (1, WINDOW, LANES), 2) < HEAD64
    qn = _norm64(q_ref[:, c0:c1], qg_ref[...] * (HEAD64 ** -0.5)).reshape(nblk, WINDOW, LANES)
    q_s = jnp.concatenate([jnp.where(lo_q, qn, 0.0), jnp.where(lo_q, 0.0, qn)],
                          axis=1).astype(BF16)
    s = jnp.where(mask, _dot(q_s, keys, BNT), NEG_BIG)
    sink = jnp.where(upper_rows, sink_ref[4 + i], sink_ref[i])
    m = jnp.maximum(jnp.max(s, axis=-1, keepdims=True), sink)
    e = jnp.exp(s - m)
    rden = 1.0 / (jnp.sum(e, axis=-1, keepdims=True) + jnp.exp(sink - m))
    e = e.astype(BF16)
    out = (_dot(e[:, 0:WINDOW, :], v_lo, BNN) * rden[:, 0:WINDOW, :]
           + _dot(e[:, WINDOW:2 * WINDOW, :], v_hi, BNN) * rden[:, WINDOW:2 * WINDOW, :])
    return out.reshape(tq, LANES) * _silu(g_ref[:, c0:c1])


def _memkv_kernel(m_ref, g_ref, w_ref, kg_ref, o_ref, h_ref):
    j = pl.program_id(0)

    @pl.when(j == 0)
    def _():
        x = m_ref[...]
        ms = jnp.mean(x * x, axis=-1, keepdims=True)
        h_ref[...] = (x * lax.rsqrt(ms + EPS) * g_ref[...]).astype(BF16)

    kv = _dot(h_ref[...], w_ref[...])

    @pl.when(j < MEM_HEADS)
    def _():
        ms = jnp.mean(kv * kv, axis=-1, keepdims=True)
        o_ref[...] = kv * lax.rsqrt(ms + EPS) * kg_ref[...]

    @pl.when(j >= MEM_HEADS)
    def _():
        o_ref[...] = kv


def _memkv(mem2, g, w_kv, kg):
    M, D = mem2.shape
    N = w_kv.shape[1]
    return pl.pallas_call(
        _memkv_kernel,
        grid=(N // LANES,),
        in_specs=[
            pl.BlockSpec((M, D), lambda j: (0, 0)),
            pl.BlockSpec((1, D), lambda j: (0, 0)),
            pl.BlockSpec((D, LANES), lambda j: (0, j)),
            pl.BlockSpec((1, LANES), lambda j: (0, 0)),
        ],
        out_specs=pl.BlockSpec((M, LANES), lambda j: (0, j)),
        out_shape=jax.ShapeDtypeStruct((M, N), F32),
        scratch_shapes=[pltpu.VMEM((M, D), BF16)],
        compiler_params=pltpu.CompilerParams(
            dimension_semantics=("arbitrary",), vmem_limit_bytes=VMEM_LIMIT),
        name="memkv",
    )(mem2, g, w_kv, kg)


def _mem_head(h, q_ref, g_ref, kv_ref, qg_ref):
    head = GROUP_W // MEM_HEADS
    c0, c1 = h * head, (h + 1) * head
    q = q_ref[:, c0:c1]
    gain = qg_ref[...] * (head ** -0.5)
    qn = q * lax.rsqrt(jnp.mean(q * q, axis=-1, keepdims=True) + EPS) * gain
    s = _mm(qn, kv_ref[:, c0:c1], NT)
    m = jnp.max(s, axis=-1, keepdims=True)
    e = jnp.exp(s - m)
    rden = 1.0 / jnp.sum(e, axis=-1, keepdims=True)
    out = _mm(e, kv_ref[:, GROUP_W + c0:GROUP_W + c1]) * rden
    return out * _silu(g_ref[:, c0:c1])


def _post_kernel(sink_ref, x_ref, orw_ref, lx_ref, lg_ref, sq_ref, skv_ref, skvp_ref, sg_ref,
                 mq_ref, mg_ref, cw_ref, cb_ref, wa_ref, ba_ref, wx_ref, bx_ref, lam_ref,
                 sqg_ref, skg_ref, mkv_ref, mqg_ref, w_ref, y_ref, xbuf_ref, h_ref):
    @pl.when(pl.program_id(0) == 0)
    def _():
        xbuf_ref[0:8, :] = jnp.zeros((8, GROUP_W), F32)
        h_ref[...] = jnp.zeros_like(h_ref)

    n_stage = GROUP_W // LANES
    piece = y_ref.shape[1] // n_stage

    def project(o16, group, j):
        return _dot(o16, w_ref[group * GROUP_W:(group + 1) * GROUP_W, j * piece:(j + 1) * piece])

    def accumulate(stage_fn, o16, group, first=False):
        cols = []
        for j in range(n_stage):
            cols.append(stage_fn(j))
            n0, n1 = j * piece, (j + 1) * piece
            base = x_ref[:, n0:n1] if first else y_ref[:, n0:n1]
            y_ref[:, n0:n1] = base + project(o16, group, j)
        return jnp.concatenate(cols, axis=1).astype(BF16)

    lru = functools.partial(_lru_block, x_ref=lx_ref, g_ref=lg_ref, cw_ref=cw_ref, cb_ref=cb_ref,
                            wa_ref=wa_ref, ba_ref=ba_ref, wx_ref=wx_ref, bx_ref=bx_ref,
                            lam_ref=lam_ref, xbuf_ref=xbuf_ref, h_ref=h_ref)
    o_lru = accumulate(lru, orw_ref[...].astype(BF16), 1, first=True)
    prepared = _swa_prepare(skv_ref, skvp_ref, skg_ref)
    swa = functools.partial(_swa_block, prepared=prepared, sink_ref=sink_ref, q_ref=sq_ref,
                            g_ref=sg_ref, qg_ref=sqg_ref)
    o_swa = accumulate(swa, o_lru, 0)
    mem = functools.partial(_mem_head, q_ref=mq_ref, g_ref=mg_ref, kv_ref=mkv_ref, qg_ref=mqg_ref)
    o_mem = accumulate(mem, o_swa, 2)
    for j in range(n_stage):
        n0, n1 = j * piece, (j + 1) * piece
        y_ref[:, n0:n1] = y_ref[:, n0:n1] + project(o_mem, 3, j)


def _post(x2, p, o_rw, sinks, lru_prm, swa_prm, mem_prm, w_out, tm):
    S, D = x2.shape
    nblk = tm // WINDOW
    col = lambda off, w=GROUP_W: pl.BlockSpec((tm, w), lambda i: (i, off // w))
    const = lambda shape: pl.BlockSpec(shape, lambda i: (0,) * len(shape))
    cw, cb, wa_bd, ba, wx_bd, bx, lam = lru_prm
    qg2, kg2 = swa_prm
    kv, mqg = mem_prm
    return pl.pallas_call(
        _post_kernel,
        grid=(S // tm,),
        in_specs=[
            pl.BlockSpec(memory_space=pltpu.SMEM),
            pl.BlockSpec((tm, D), lambda i: (i, 0)),
            pl.BlockSpec((tm, GROUP_W), lambda i: (i, 0)),
            col(COL_LRU_X), col(COL_LRU_G),
            col(COL_SWA_Q), col(COL_SWA_KV, 2 * LANES),
            pl.BlockSpec((WINDOW, 2 * LANES),
                         lambda i: (jnp.maximum(i * nblk - 1, 0), COL_SWA_KV // (2 * LANES))),
            col(COL_SWA_G), col(COL_MEM_Q), col(COL_MEM_G),
            const((CONV_W, GROUP_W)), const((1, GROUP_W)),
            const((GROUP_W, GROUP_W)), const((1, GROUP_W)),
            const((GROUP_W, GROUP_W)), const((1, GROUP_W)), const((1, GROUP_W)),
            const((1, LANES)), const((1, LANES)),
            const((N_MEM, 2 * GROUP_W)), const((1, GROUP_W // MEM_HEADS)),
            pl.BlockSpec((4 * GROUP_W, D), lambda i: (0, 0), pipeline_mode=pl.Buffered(1)),
        ],
        out_specs=pl.BlockSpec((tm, D), lambda i: (i, 0)),
        out_shape=jax.ShapeDtypeStruct((S, D), F32),
        scratch_shapes=[pltpu.VMEM((tm + 8, GROUP_W), F32), pltpu.VMEM((1, GROUP_W), F32)],
        compiler_params=pltpu.CompilerParams(
            dimension_semantics=("arbitrary",), vmem_limit_bytes=VMEM_LIMIT),
        name="post",
    )(sinks, x2, o_rw, p, p, p, p, p, p, p, p, cw, cb, wa_bd, ba, wx_bd, bx, lam,
      qg2, kg2, kv, mqg, w_out)


SWA_PERM = (0, 4, 1, 5, 2, 6, 3, 7)


def _perm_heads(t, axis):
    parts = jnp.split(t, 8, axis=axis)
    return jnp.concatenate([parts[h] for h in SWA_PERM], axis=axis)


def _wprep_kernel(wt_ref, o_ref):
    tk = wt_ref.shape[1]

    def put(dst, src_rows):
        col = dst
        for start, count in src_rows:
            for r in range(0, count, LANES):
                n = min(LANES, count - r)
                o_ref[:, col:col + n] = wt_ref[start + r:start + r + n, :].T.astype(BF16)
                col += n

    put(0, [(0, IN_HEAD)])
    src_g_rw = IN_HEAD + 2 * LORA
    src_q = src_g_rw + GROUP_W
    src_kv = src_q + GROUP_W
    src_g = src_kv + 2 * LANES
    put(COL_RW_G, [(src_g_rw, GROUP_W)])
    for base, dst in ((src_q, COL_SWA_Q), (src_g, COL_SWA_G)):
        for i in range(GROUP_W // LANES):
            put(dst + i * LANES, [(base + HEAD64 * i, HEAD64), (base + HEAD64 * (4 + i), HEAD64)])
    put(COL_MEM_Q, [(src_g + GROUP_W, 2 * GROUP_W)])
    put(COL_SWA_KV, [(src_kv, 2 * LANES)])
    put(COL_RW_LORA, [(IN_HEAD, 2 * LORA)])
    pad0 = COL_RW_LORA + 2 * LORA
    o_ref[:, pad0:IN_PAD] = jnp.zeros((tk, IN_PAD - pad0), BF16)


def _wprep(w_in, tk=256):
    L, D, N = w_in.shape
    return pl.pallas_call(
        _wprep_kernel,
        grid=(L, D // tk),
        in_specs=[pl.BlockSpec((None, N, tk), lambda l, i: (l, 0, i))],
        out_specs=pl.BlockSpec((None, tk, IN_PAD), lambda l, i: (l, i, 0)),
        out_shape=jax.ShapeDtypeStruct((L, D, IN_PAD), BF16),
        compiler_params=pltpu.CompilerParams(
            dimension_semantics=("arbitrary", "arbitrary"), vmem_limit_bytes=VMEM_LIMIT),
        name="wprep",
    )(jnp.swapaxes(w_in, 1, 2))


def _block_diag(w):
    nb, bw, _ = w.shape
    eye = jnp.eye(nb, dtype=w.dtype)
    return (eye[:, None, :, None] * w[:, :, None, :]).reshape(nb * bw, nb * bw)


def _layer(x2, mem2, prm, w_pad_all, layer, tiles, passes):
    (norm_g, w_in, conv_w, conv_b, lru_wa, lru_ba, lru_wx, lru_bx, lru_lambda,
     rw_mu, rw_w0, rw_w_up, rw_a0, rw_a_up, rw_k_k, rw_k_a, rw_r_k, rw_gn_g, rw_gn_b,
     swa_q_g, swa_k_g, swa_sinks, mem_norm_g, w_mem_kv, mem_q_g, mem_k_g, w_out) = prm
    row = lambda t: t.reshape(1, -1)

    p = _inproj(x2, row(norm_g), w_pad_all, layer, tiles["inproj"])

    mu_pad = jnp.concatenate([rw_mu, jnp.zeros((LANES - 2 * LORA,), F32)]).reshape(1, -1)
    zpad = jnp.zeros((LANES - LORA, GROUP_W), F32)
    wup_pad = jnp.concatenate([rw_w_up, zpad], axis=0)
    aup_pad = jnp.concatenate([jnp.zeros((LORA, GROUP_W), F32), rw_a_up,
                               jnp.zeros((LANES - 2 * LORA, GROUP_W), F32)], axis=0)
    o_rw = _rwkv(p, mu_pad, row(rw_w0), wup_pad, row(rw_a0), aup_pad, row(rw_k_k), row(rw_k_a),
                 row(rw_r_k), row(rw_gn_g), row(rw_gn_b), tiles["rwkv"], passes)

    kv = _memkv(mem2, row(mem_norm_g), w_mem_kv.astype(BF16), row(mem_k_g))
    two = lambda t: jnp.concatenate([t, t]).reshape(1, LANES)
    w_out_p = jnp.concatenate(
        [w_out[0:2 * GROUP_W], _perm_heads(w_out[2 * GROUP_W:3 * GROUP_W], 0), w_out[3 * GROUP_W:]],
        axis=0).astype(BF16)
    lru_prm = (conv_w, row(conv_b), _block_diag(lru_wa).astype(BF16), row(lru_ba),
               _block_diag(lru_wx).astype(BF16), row(lru_bx), row(lru_lambda))
    return _post(x2, p, o_rw, swa_sinks, lru_prm, (two(swa_q_g), two(swa_k_g)),
                 (kv, row(mem_q_g)), w_out_p, tiles["post"])


def _tiles(S):
    pick = lambda t: t if S % t == 0 else S
    return {"inproj": pick(256), "rwkv": pick(512), "post": pick(512)}


def kernel(x, mem, norm_g, w_in, conv_w, conv_b, lru_wa, lru_ba, lru_wx, lru_bx, lru_lambda, rw_mu, rw_w0, rw_w_up, rw_a0, rw_a_up, rw_k_k, rw_k_a, rw_r_k, rw_gn_g, rw_gn_b, swa_q_g, swa_k_g, swa_sinks, mem_norm_g, w_mem_kv, mem_q_g, mem_k_g, w_out):
    B, S, D = x.shape
    assert B == 1
    params = (norm_g, w_in, conv_w, conv_b, lru_wa, lru_ba, lru_wx, lru_bx, lru_lambda,
              rw_mu, rw_w0, rw_w_up, rw_a0, rw_a_up, rw_k_k, rw_k_a, rw_r_k, rw_gn_g, rw_gn_b,
              swa_q_g, swa_k_g, swa_sinks, mem_norm_g, w_mem_kv, mem_q_g, mem_k_g, w_out)
    x2 = x.reshape(S, D)
    mem2 = mem.reshape(mem.shape[1], D)
    tiles = _tiles(S)
    w_pad_all = _wprep(w_in)
    for l in range(norm_g.shape[0]):
        x2 = _layer(x2, mem2, tuple(t[l] for t in params), w_pad_all, l, tiles, passes=1)
    return x2.reshape(B, S, D)
```

```python
import functools

import jax
import jax.numpy as jnp
from jax import lax
from jax.experimental import pallas as pl
from jax.experimental.pallas import tpu as pltpu

F32 = jnp.float32
BF16 = jnp.bfloat16

LANES = 128
SUBLANES = 8
GROUP_W = 512
HEAD64 = 64
CHUNK = 64
CONV_W = 4
LRU_C = 8.0
LORA = 32
EPS = 1e-6
RWKV_GN_EPS = 64e-5
WINDOW = 128
N_MEM = 256
MEM_HEADS = 4
NEG_BIG = -1e30
VMEM_LIMIT = 56 * 1024 * 1024

COL_LRU_X, COL_LRU_G = 0, 512
COL_RW_R, COL_RW_K, COL_RW_V, COL_RW_G = 1024, 1536, 2048, 2560
COL_SWA_Q, COL_SWA_G = 3072, 3584
COL_MEM_Q, COL_MEM_G = 4096, 4608
COL_SWA_KV, COL_RW_LORA = 5120, 5376
IN_PAD = 5632
IN_HEAD = 2560

NN = (((1,), (0,)), ((), ()))
NT = (((1,), (1,)), ((), ()))
TN = (((0,), (0,)), ((), ()))
BNN = (((2,), (1,)), ((0,), (0,)))
BNT = (((2,), (2,)), ((0,), (0,)))
BTN = (((1,), (1,)), ((0,), (0,)))


def _dot(a, b, dims=NN):
    return lax.dot_general(a, b, dims, preferred_element_type=F32)


def _split(a):
    hi = a.astype(BF16)
    lo = (a - hi.astype(F32)).astype(BF16)
    return hi, lo


def _mm(a, b, dims=NN, passes=1):
    if passes == 1:
        return _dot(a.astype(BF16), b.astype(BF16), dims)
    ah, al = _split(a)
    bh, bl = _split(b)
    return _dot(ah, bh, dims) + (_dot(ah, bl, dims) + _dot(al, bh, dims))


def _mm_exact_rhs(a, b_bf16):
    ah, al = _split(a)
    return _dot(ah, b_bf16) + _dot(al, b_bf16)


def _iota2(shape, axis):
    return lax.broadcasted_iota(jnp.int32, shape, axis)


def _head_ones():
    r = _iota2((LANES, LANES), 0) // HEAD64
    c = _iota2((LANES, LANES), 1) // HEAD64
    return (r == c).astype(BF16)


def _segsum64(x, ones_bd):
    n, w = x.shape
    cols = [_mm_exact_rhs(x[:, j:j + LANES], ones_bd) for j in range(0, w, LANES)]
    return cols[0] if len(cols) == 1 else jnp.concatenate(cols, axis=1)


def _norm64(x, gain):
    ss = _segsum64(x * x, _head_ones())
    return x * lax.rsqrt(ss * (1.0 / HEAD64) + EPS) * gain


def _softplus(z):
    return jnp.maximum(z, 0.0) + jnp.log1p(jnp.exp(-jnp.abs(z)))


def _sigmoid(z):
    return 0.5 * jnp.tanh(0.5 * z) + 0.5


def _silu(z):
    return z * _sigmoid(z)


def _inproj_kernel(x_ref, g_ref, w_ref, o_ref):
    x = x_ref[...]
    ms = jnp.mean(x * x, axis=-1, keepdims=True)
    h = (x * lax.rsqrt(ms + EPS) * g_ref[...]).astype(BF16)
    o_ref[...] = _dot(h, w_ref[...])


def _inproj(x2, g, w_pad_all, layer, tm):
    S, D = x2.shape
    return pl.pallas_call(
        _inproj_kernel,
        grid=(S // tm,),
        in_specs=[
            pl.BlockSpec((tm, D), lambda i: (i, 0)),
            pl.BlockSpec((1, D), lambda i: (0, 0)),
            pl.BlockSpec((None, D, IN_PAD), lambda i: (layer, 0, 0),
                         pipeline_mode=pl.Buffered(1)),
        ],
        out_specs=pl.BlockSpec((tm, IN_PAD), lambda i: (i, 0)),
        out_shape=jax.ShapeDtypeStruct((S, IN_PAD), F32),
        compiler_params=pltpu.CompilerParams(
            dimension_semantics=("arbitrary",), vmem_limit_bytes=VMEM_LIMIT),
        name="inproj",
    )(x2, g, w_pad_all)


def _lru_block(j, x_ref, g_ref, cw_ref, cb_ref, wa_ref, ba_ref, wx_ref, bx_ref, lam_ref,
               xbuf_ref, h_ref):
    tm = x_ref.shape[0]
    c0, c1 = j * LANES, (j + 1) * LANES
    x = x_ref[:, c0:c1]
    xbuf_ref[8:8 + tm, c0:c1] = x
    conv = cb_ref[:, c0:c1] + cw_ref[CONV_W - 1:CONV_W, c0:c1] * x
    for t in range(CONV_W - 1):
        off = 8 - (CONV_W - 1) + t
        conv = conv + cw_ref[t:t + 1, c0:c1] * xbuf_ref[off:off + tm, c0:c1]
    xbuf_ref[0:8, c0:c1] = x[tm - 8:tm, :]

    cb16 = conv.astype(BF16)
    r = _sigmoid(_dot(cb16, wa_ref[c0:c1, c0:c1]) + ba_ref[:, c0:c1])
    ig = _sigmoid(_dot(cb16, wx_ref[c0:c1, c0:c1]) + bx_ref[:, c0:c1])
    log_a = (-LRU_C) * r * _softplus(-lam_ref[:, c0:c1])
    a = jnp.exp(log_a)
    m2 = jnp.maximum(-jnp.tanh(log_a) * (a * a + 1.0), 1e-12)
    u = (m2 * lax.rsqrt(m2)) * (ig * conv)

    n_slabs = tm // SUBLANES
    a3 = a.reshape(n_slabs, SUBLANES, LANES)
    u3 = u.reshape(n_slabs, SUBLANES, LANES)
    sub = lax.broadcasted_iota(jnp.int32, (1, SUBLANES, LANES), 1)
    d = 1
    while d < SUBLANES:
        keep = sub >= d
        a_sh = jnp.where(keep, pltpu.roll(a3, d, 1), 1.0)
        u_sh = jnp.where(keep, pltpu.roll(u3, d, 1), 0.0)
        u3 = u3 + a3 * u_sh
        a3 = a3 * a_sh
        d *= 2
    h = h_ref[:, c0:c1]
    slabs = []
    for s in range(n_slabs):
        h_s = u3[s] + a3[s] * h
        slabs.append(h_s)
        h = h_s[SUBLANES - 1:SUBLANES, :]
    h_ref[:, c0:c1] = h
    return jnp.concatenate(slabs, axis=0) * _silu(g_ref[:, c0:c1])


def _rwkv_rows(r0, tm, r_ref, k_ref, v_ref, l_ref, g_ref, mu_ref, w0_ref, wup_ref, a0_ref,
               aup_ref, kk_ref, ka_ref, rk_ref, gng_ref, gnb_ref, o_ref, prev_ref, state_ref, passes):
    n_chunks = tm // CHUNK
    n_pairs = GROUP_W // LANES
    WIN = 3 * GROUP_W + LANES
    row_t = _iota2((tm, 1), 0)
    last = r0 + tm == r_ref.shape[0]

    def shift_lerp(ref, c0, c1):
        x = ref[r0:r0 + tm, :]
        before = prev_ref[:, c0:c1] if r0 == 0 else ref[r0 - 1:r0, :]
        xs = jnp.where(row_t == 0, before, pltpu.roll(x, 1, 0))
        if last:
            prev_ref[:, c0:c1] = x[tm - 1:tm, :]
        return x + (xs - x) * mu_ref[:, c0:c1]

    r = shift_lerp(r_ref, 0, GROUP_W)
    k = shift_lerp(k_ref, GROUP_W, 2 * GROUP_W)
    v = shift_lerp(v_ref, 2 * GROUP_W, 3 * GROUP_W)
    lo_in = shift_lerp(l_ref, 3 * GROUP_W, WIN)

    lane_l = _iota2((tm, LANES), 1)
    lora = jnp.where(lane_l < LORA, jnp.tanh(lo_in), lo_in)
    w = -_softplus(-(w0_ref[...] + _mm(lora, wup_ref[...], passes=3))) - 0.5
    lw = -jnp.exp(w)
    a = _sigmoid(a0_ref[...] + _mm(lora, aup_ref[...], passes=3))
    yield

    ones_bd = _head_ones()
    kk = k * kk_ref[...]
    kk = kk * jnp.minimum(lax.rsqrt(_segsum64(kk * kk, ones_bd)), 1e12)
    k2 = k * (1.0 + (a - 1.0) * ka_ref[...])
    b = kk * a
    bonus = _segsum64(r * k2 * rk_ref[...], ones_bd) * v
    yield

    cl = lw
    rows_c = _iota2((tm, GROUP_W), 0) % CHUNK
    d = 1
    while d < CHUNK:
        cl = cl + jnp.where(rows_c >= d, pltpu.roll(cl, d, 0), 0.0)
        d *= 2
    yield

    to3 = lambda t: t.reshape(n_chunks, CHUNK, GROUP_W)
    cl3 = to3(cl)
    cl_end = cl3[:, CHUNK - 1:CHUNK, :]
    e_inv = jnp.exp(-cl3)
    e_end = jnp.exp(cl_end - cl3)
    w_end = jnp.exp(cl_end)
    kk3, b3, k23 = to3(kk), to3(b), to3(k2)

    lo_lane = lax.broadcasted_iota(jnp.int32, (1, CHUNK, LANES), 2) < HEAD64
    n_units = n_chunks * n_pairs
    op_dtype = BF16 if passes == 1 else F32
    cast = lambda t: t.astype(op_dtype)
    mm = functools.partial(_mm, passes=passes)

    def natural(t3):
        units = [t3[:, :, p * LANES:(p + 1) * LANES] for p in range(n_pairs)]
        return jnp.stack(units, axis=1).reshape(n_units, t3.shape[1], LANES)

    def stacked(x):
        x = cast(x)
        zero = jnp.zeros_like(x)
        return jnp.concatenate([jnp.where(lo_lane, x, zero), jnp.where(lo_lane, zero, x)], axis=1)

    a_n = natural(-kk3 * jnp.exp(cl3 - to3(lw)))
    r_n = natural(to3(r) * jnp.exp(cl3))
    a_s = stacked(a_n)
    b_s = stacked(natural(b3 * e_inv))
    k_s = stacked(natural(k23 * e_inv))
    yield
    be_s = stacked(natural(b3 * e_end))
    ke_s = stacked(natural(k23 * e_end))
    v_s = stacked(natural(to3(v)))
    wend_u = natural(w_end)
    yield

    ti = lax.broadcasted_iota(jnp.int32, (1, CHUNK, LANES), 1)
    si = lax.broadcasted_iota(jnp.int32, (1, CHUNK, LANES), 2) % CHUNK
    strict = si < ti
    incl = si <= ti
    diag16 = strict & ((ti // 16) == (si // 16))
    off32 = strict & ((ti // 32) == (si // 32)) & ((ti // 16) != (si // 16))
    off64 = strict & ((ti // 32) != (si // 32))
    eye = (si == ti).astype(F32)

    g = mm(cast(jnp.concatenate([a_n, r_n], axis=1)), jnp.concatenate([b_s, k_s], axis=1), BNT)
    g_ab, g_ak = g[:, 0:CHUNK, 0:LANES], g[:, 0:CHUNK, LANES:2 * LANES]
    g_rb, g_rk = g[:, CHUNK:2 * CHUNK, 0:LANES], g[:, CHUNK:2 * CHUNK, LANES:2 * LANES]
    l_ak = cast(jnp.where(strict, g_ak, 0.0))
    m_rb = cast(jnp.where(incl, g_rb, 0.0))
    m_rk = cast(jnp.where(incl, g_rk, 0.0))
    yield

    d1 = jnp.where(diag16, g_ab, 0.0)
    d1c = cast(d1)
    d2 = cast(mm(d1c, stacked(d1c), BNN))
    d2_s = stacked(d2)
    yield
    d4 = cast(mm(d2, d2_s, BNN))
    d4_s = stacked(d4)
    yield
    d8_s = stacked(mm(d4, d4_s, BNN))
    yield
    t = eye + d1
    t = t + mm(cast(t), d2_s, BNN)
    yield
    t = t + mm(cast(t), d4_s, BNN)
    yield
    t = t + mm(cast(t), d8_s, BNN)
    yield
    for off in (off32, off64):
        tc = cast(t)
        et = mm(cast(jnp.where(off, g_ab, 0.0)), stacked(tc), BNN)
        t = t + mm(tc, stacked(et), BNN)
        yield

    lv = mm(l_ak, v_s, BNN)
    yield
    tu = mm(cast(t), jnp.concatenate([a_s, stacked(lv)], axis=2), BNN)
    tu_s = jnp.concatenate([stacked(tu[:, :, 0:LANES]), stacked(tu[:, :, LANES:2 * LANES])], axis=2)
    yield
    ru = mm(m_rb, tu_s, BNN)
    r_hat = cast(r_n + ru[:, :, 0:LANES])
    y_v = ru[:, :, LANES:2 * LANES] + mm(m_rk, v_s, BNN)
    yield
    pq = mm(be_s, tu_s, BTN)
    pp = cast(pq[:, :, 0:LANES])
    q = pq[:, :, LANES:2 * LANES] + mm(ke_s, v_s, BTN)
    ri = lax.broadcasted_iota(jnp.int32, (1, LANES, LANES), 1)
    ci = lax.broadcasted_iota(jnp.int32, (1, LANES, LANES), 2)
    wcol = jnp.sum(jnp.where(ri == ci, wend_u, 0.0), axis=2, keepdims=True)
    yield

    s0 = state_ref[...]
    y_parts = []
    for c in range(n_chunks):
        u0, u1 = c * n_pairs, (c + 1) * n_pairs
        s0c = cast(s0)
        y_parts.append(mm(r_hat[u0:u1], s0c, BNN) + y_v[u0:u1])
        s0 = wcol[u0:u1] * s0 + mm(pp[u0:u1], s0c, BNN) + q[u0:u1]
        yield
    state_ref[...] = s0
    y4 = jnp.concatenate(y_parts, axis=0).reshape(n_chunks, n_pairs, CHUNK, LANES)
    y = jnp.concatenate([y4[:, p] for p in range(n_pairs)], axis=-1).reshape(tm, GROUP_W)
    yield
    inv_n = 1.0 / HEAD64
    mean = _segsum64(y, ones_bd) * inv_n
    yc = y - mean
    var = _segsum64(yc * yc, ones_bd) * inv_n
    yn = yc * lax.rsqrt(var + RWKV_GN_EPS) * gng_ref[...] + gnb_ref[...]
    o_ref[r0:r0 + tm, :] = (yn + bonus) * _silu(g_ref[r0:r0 + tm, :])


N_RWKV_HEAD_STAGES = 5
RWKV_CHAIN_PER_HEAD = 1


def _rwkv_kernel(*refs, passes, ranges):
    prev_ref, state_ref = refs[-2:]

    @pl.when(pl.program_id(0) == 0)
    def _():
        prev_ref[...] = jnp.zeros_like(prev_ref)
        state_ref[...] = jnp.zeros_like(state_ref)

    tm = refs[0].shape[0] // ranges
    streams = [_rwkv_rows(n * tm, tm, *refs, passes) for n in range(ranges)]
    def advance(g, n):
        for _ in range(n):
            if next(g, "done") == "done":
                return False
        return True

    older = []
    for g in streams:
        for _ in range(N_RWKV_HEAD_STAGES):
            older = [o for o in older if advance(o, RWKV_CHAIN_PER_HEAD)]
            advance(g, 1)
        older.append(g)
    while older:
        older = [o for o in older if advance(o, 1)]


def _rwkv(p, mu, w0, wup, a0, aup, kkw, kaw, rkw, gng, gnb, tm, passes):
    S = p.shape[0]
    row = lambda n=1, w=GROUP_W: pl.BlockSpec((n, w), lambda i: (0, 0))
    col = lambda off, w=GROUP_W: pl.BlockSpec((tm, w), lambda i: (i, off // w))
    WIN = 3 * GROUP_W + LANES
    return pl.pallas_call(
        functools.partial(_rwkv_kernel, passes=passes, ranges=2 if tm % (2 * CHUNK * 2) == 0 else 1),
        grid=(S // tm,),
        in_specs=[
            col(COL_RW_R), col(COL_RW_K), col(COL_RW_V), col(COL_RW_LORA, LANES), col(COL_RW_G),
            row(1, WIN), row(), row(LANES), row(), row(LANES),
            row(), row(), row(), row(), row(),
        ],
        out_specs=pl.BlockSpec((tm, GROUP_W), lambda i: (i, 0)),
        out_shape=jax.ShapeDtypeStruct((S, GROUP_W), F32),
        scratch_shapes=[
            pltpu.VMEM((1, WIN), F32),
            pltpu.VMEM((GROUP_W // LANES, LANES, LANES), F32),
        ],
        compiler_params=pltpu.CompilerParams(
            dimension_semantics=("arbitrary",), vmem_limit_bytes=VMEM_LIMIT),
        name="rwkv7",
    )(p, p, p, p, p, mu, w0, wup, a0, aup, kkw, kaw, rkw, gng, gnb)


def _swa_prepare(kv_ref, kvp_ref, kg_ref):
    tq = kv_ref.shape[0]
    nblk = tq // WINDOW
    kv_all = jnp.concatenate([kvp_ref[...], kv_ref[...]], axis=0)
    k_all = _norm64(kv_all[:, 0:LANES], kg_ref[...]).astype(BF16)
    v_all = kv_all[:, LANES:2 * LANES]
    lo_v = _iota2((WINDOW + tq, LANES), 1) < HEAD64
    v_lo_all = jnp.where(lo_v, v_all, 0.0).astype(BF16)
    v_hi_all = jnp.where(lo_v, 0.0, v_all).astype(BF16)

    window = lambda t: jnp.stack(
        [t[b * WINDOW:(b + 2) * WINDOW, :] for b in range(nblk)], axis=0)

    shape3 = (nblk, 2 * WINDOW, 2 * WINDOW)
    qi = lax.broadcasted_iota(jnp.int32, shape3, 1) % WINDOW
    kj = lax.broadcasted_iota(jnp.int32, shape3, 2)
    blk = lax.broadcasted_iota(jnp.int32, shape3, 0)
    has_prev = (kj >= WINDOW) | (blk > 0) | (pl.program_id(0) > 0)
    mask = (kj > qi) & (kj <= qi + WINDOW) & has_prev
    return window(k_all), window(v_lo_all), window(v_hi_all), mask


def _swa_block(i, prepared, sink_ref, q_ref, g_ref, qg_ref):
    keys, v_lo, v_hi, mask = prepared
    tq = q_ref.shape[0]
    nblk = tq // WINDOW
    c0, c1 = i * LANES, (i + 1) * LANES
    upper_rows = lax.broadcasted_iota(jnp.int32, (1, 2 * WINDOW, 1), 1) >= WINDOW
    lo_q = lax.broadcasted_iota(jnp.int32, (1, WINDOW, LANES), 2) < HEAD64
    qn = _norm64(q_ref[:, c0:c1], qg_ref[...] * (HEAD64 ** -0.5)).reshape(nblk, WINDOW, LANES)
    q_s = jnp.concatenate([jnp.where(lo_q, qn, 0.0), jnp.where(lo_q, 0.0, qn)],
                          axis=1).astype(BF16)
    s = jnp.where(mask, _dot(q_s, keys, BNT), NEG_BIG)
    sink = jnp.where(upper_rows, sink_ref[4 + i], sink_ref[i])
    m = jnp.maximum(jnp.max(s, axis=-1, keepdims=True), sink)
    e = jnp.exp(s - m)
    rden = 1.0 / (jnp.sum(e, axis=-1, keepdims=True) + jnp.exp(sink - m))
    e = e.astype(BF16)
    out = (_dot(e[:, 0:WINDOW, :], v_lo, BNN) * rden[:, 0:WINDOW, :]
           + _dot(e[:, WINDOW:2 * WINDOW, :], v_hi, BNN) * rden[:, WINDOW:2 * WINDOW, :])
    return out.reshape(tq, LANES) * _silu(g_ref[:, c0:c1])


def _memkv_kernel(m_ref, g_ref, w_ref, kg_ref, o_ref, h_ref):
    j = pl.program_id(0)

    @pl.when(j == 0)
    def _():
        x = m_ref[...]
        ms = jnp.mean(x * x, axis=-1, keepdims=True)
        h_ref[...] = (x * lax.rsqrt(ms + EPS) * g_ref[...]).astype(BF16)

    kv = _dot(h_ref[...], w_ref[...])

    @pl.when(j < MEM_HEADS)
    def _():
        ms = jnp.mean(kv * kv, axis=-1, keepdims=True)
        o_ref[...] = kv * lax.rsqrt(ms + EPS) * kg_ref[...]

    @pl.when(j >= MEM_HEADS)
    def _():
        o_ref[...] = kv


def _memkv(mem2, g, w_kv, kg):
    M, D = mem2.shape
    N = w_kv.shape[1]
    return pl.pallas_call(
        _memkv_kernel,
        grid=(N // LANES,),
        in_specs=[
            pl.BlockSpec((M, D), lambda j: (0, 0)),
            pl.BlockSpec((1, D), lambda j: (0, 0)),
            pl.BlockSpec((D, LANES), lambda j: (0, j)),
            pl.BlockSpec((1, LANES), lambda j: (0, 0)),
        ],
        out_specs=pl.BlockSpec((M, LANES), lambda j: (0, j)),
        out_shape=jax.ShapeDtypeStruct((M, N), F32),
        scratch_shapes=[pltpu.VMEM((M, D), BF16)],
        compiler_params=pltpu.CompilerParams(
            dimension_semantics=("arbitrary",), vmem_limit_bytes=VMEM_LIMIT),
        name="memkv",
    )(mem2, g, w_kv, kg)


def _mem_head(h, q_ref, g_ref, kv_ref, qg_ref):
    head = GROUP_W // MEM_HEADS
    c0, c1 = h * head, (h + 1) * head
    q = q_ref[:, c0:c1]
    gain = qg_ref[...] * (head ** -0.5)
    qn = q * lax.rsqrt(jnp.mean(q * q, axis=-1, keepdims=True) + EPS) * gain
    s = _mm(qn, kv_ref[:, c0:c1], NT)
    m = jnp.max(s, axis=-1, keepdims=True)
    e = jnp.exp(s - m)
    rden = 1.0 / jnp.sum(e, axis=-1, keepdims=True)
    out = _mm(e, kv_ref[:, GROUP_W + c0:GROUP_W + c1]) * rden
    return out * _silu(g_ref[:, c0:c1])


def _post_kernel(sink_ref, x_ref, orw_ref, lx_ref, lg_ref, sq_ref, skv_ref, skvp_ref, sg_ref,
                 mq_ref, mg_ref, cw_ref, cb_ref, wa_ref, ba_ref, wx_ref, bx_ref, lam_ref,
                 sqg_ref, skg_ref, mkv_ref, mqg_ref, w_ref, y_ref, xbuf_ref, h_ref):
    @pl.when(pl.program_id(0) == 0)
    def _():
        xbuf_ref[0:8, :] = jnp.zeros((8, GROUP_W), F32)
        h_ref[...] = jnp.zeros_like(h_ref)

    n_stage = GROUP_W // LANES
    piece = y_ref.shape[1] // n_stage

    def project(o16, group, j):
        return _dot(o16, w_ref[group * GROUP_W:(group + 1) * GROUP_W, j * piece:(j + 1) * piece])

    def accumulate(stage_fn, o16, group, first=False):
        cols = []
        for j in range(n_stage):
            cols.append(stage_fn(j))
            n0, n1 = j * piece, (j + 1) * piece
            base = x_ref[:, n0:n1] if first else y_ref[:, n0:n1]
            y_ref[:, n0:n1] = base + project(o16, group, j)
        return jnp.concatenate(cols, axis=1).astype(BF16)

    lru = functools.partial(_lru_block, x_ref=lx_ref, g_ref=lg_ref, cw_ref=cw_ref, cb_ref=cb_ref,
                            wa_ref=wa_ref, ba_ref=ba_ref, wx_ref=wx_ref, bx_ref=bx_ref,
                            lam_ref=lam_ref, xbuf_ref=xbuf_ref, h_ref=h_ref)
    o_lru = accumulate(lru, orw_ref[...].astype(BF16), 1, first=True)
    prepared = _swa_prepare(skv_ref, skvp_ref, skg_ref)
    swa = functools.partial(_swa_block, prepared=prepared, sink_ref=sink_ref, q_ref=sq_ref,
                            g_ref=sg_ref, qg_ref=sqg_ref)
    o_swa = accumulate(swa, o_lru, 0)
    mem = functools.partial(_mem_head, q_ref=mq_ref, g_ref=mg_ref, kv_ref=mkv_ref, qg_ref=mqg_ref)
    o_mem = accumulate(mem, o_swa, 2)
    for j in range(n_stage):
        n0, n1 = j * piece, (j + 1) * piece
        y_ref[:, n0:n1] = y_ref[:, n0:n1] + project(o_mem, 3, j)


def _post(x2, p, o_rw, sinks, lru_prm, swa_prm, mem_prm, w_out, tm):
    S, D = x2.shape
    nblk = tm // WINDOW
    col = lambda off, w=GROUP_W: pl.BlockSpec((tm, w), lambda i: (i, off // w))
    const = lambda shape: pl.BlockSpec(shape, lambda i: (0,) * len(shape))
    cw, cb, wa_bd, ba, wx_bd, bx, lam = lru_prm
    qg2, kg2 = swa_prm
    kv, mqg = mem_prm
    return pl.pallas_call(
        _post_kernel,
        grid=(S // tm,),
        in_specs=[
            pl.BlockSpec(memory_space=pltpu.SMEM),
            pl.BlockSpec((tm, D), lambda i: (i, 0)),
            pl.BlockSpec((tm, GROUP_W), lambda i: (i, 0)),
            col(COL_LRU_X), col(COL_LRU_G),
            col(COL_SWA_Q), col(COL_SWA_KV, 2 * LANES),
            pl.BlockSpec((WINDOW, 2 * LANES),
                         lambda i: (jnp.maximum(i * nblk - 1, 0), COL_SWA_KV // (2 * LANES))),
            col(COL_SWA_G), col(COL_MEM_Q), col(COL_MEM_G),
            const((CONV_W, GROUP_W)), const((1, GROUP_W)),
            const((GROUP_W, GROUP_W)), const((1, GROUP_W)),
            const((GROUP_W, GROUP_W)), const((1, GROUP_W)), const((1, GROUP_W)),
            const((1, LANES)), const((1, LANES)),
            const((N_MEM, 2 * GROUP_W)), const((1, GROUP_W // MEM_HEADS)),
            pl.BlockSpec((4 * GROUP_W, D), lambda i: (0, 0), pipeline_mode=pl.Buffered(1)),
        ],
        out_specs=pl.BlockSpec((tm, D), lambda i: (i, 0)),
        out_shape=jax.ShapeDtypeStruct((S, D), F32),
        scratch_shapes=[pltpu.VMEM((tm + 8, GROUP_W), F32), pltpu.VMEM((1, GROUP_W), F32)],
        compiler_params=pltpu.CompilerParams(
            dimension_semantics=("arbitrary",), vmem_limit_bytes=VMEM_LIMIT),
        name="post",
    )(sinks, x2, o_rw, p, p, p, p, p, p, p, p, cw, cb, wa_bd, ba, wx_bd, bx, lam,
      qg2, kg2, kv, mqg, w_out)


SWA_PERM = (0, 4, 1, 5, 2, 6, 3, 7)


def _perm_heads(t, axis):
    parts = jnp.split(t, 8, axis=axis)
    return jnp.concatenate([parts[h] for h in SWA_PERM], axis=axis)


def _wprep_kernel(wt_ref, o_ref):
    tk = wt_ref.shape[1]

    def put(dst, src_rows):
        col = dst
        for start, count in src_rows:
            for r in range(0, count, LANES):
                n = min(LANES, count - r)
                o_ref[:, col:col + n] = wt_ref[start + r:start + r + n, :].T.astype(BF16)
                col += n

    put(0, [(0, IN_HEAD)])
    src_g_rw = IN_HEAD + 2 * LORA
    src_q = src_g_rw + GROUP_W
    src_kv = src_q + GROUP_W
    src_g = src_kv + 2 * LANES
    put(COL_RW_G, [(src_g_rw, GROUP_W)])
    for base, dst in ((src_q, COL_SWA_Q), (src_g, COL_SWA_G)):
        for i in range(GROUP_W // LANES):
            put(dst + i * LANES, [(base + HEAD64 * i, HEAD64), (base + HEAD64 * (4 + i), HEAD64)])
    put(COL_MEM_Q, [(src_g + GROUP_W, 2 * GROUP_W)])
    put(COL_SWA_KV, [(src_kv, 2 * LANES)])
    put(COL_RW_LORA, [(IN_HEAD, 2 * LORA)])
    pad0 = COL_RW_LORA + 2 * LORA
    o_ref[:, pad0:IN_PAD] = jnp.zeros((tk, IN_PAD - pad0), BF16)


def _wprep(w_in, tk=256):
    L, D, N = w_in.shape
    return pl.pallas_call(
        _wprep_kernel,
        grid=(L, D // tk),
        in_specs=[pl.BlockSpec((None, N, tk), lambda l, i: (l, 0, i))],
        out_specs=pl.BlockSpec((None, tk, IN_PAD), lambda l, i: (l, i, 0)),
        out_shape=jax.ShapeDtypeStruct((L, D, IN_PAD), BF16),
        compiler_params=pltpu.CompilerParams(
            dimension_semantics=("arbitrary", "arbitrary"), vmem_limit_bytes=VMEM_LIMIT),
        name="wprep",
    )(jnp.swapaxes(w_in, 1, 2))


def _block_diag(w):
    nb, bw, _ = w.shape
    eye = jnp.eye(nb, dtype=w.dtype)
    return (eye[:, None, :, None] * w[:, :, None, :]).reshape(nb * bw, nb * bw)


def _layer(x2, mem2, prm, w_pad_all, layer, tiles, passes):
    (norm_g, w_in, conv_w, conv_b, lru_wa, lru_ba, lru_wx, lru_bx, lru_lambda,
     rw_mu, rw_w0, rw_w_up, rw_a0, rw_a_up, rw_k_k, rw_k_a, rw_r_k, rw_gn_g, rw_gn_b,
     swa_q_g, swa_k_g, swa_sinks, mem_norm_g, w_mem_kv, mem_q_g, mem_k_g, w_out) = prm
    row = lambda t: t.reshape(1, -1)

    p = _inproj(x2, row(norm_g), w_pad_all, layer, tiles["inproj"])

    mu_pad = jnp.concatenate([rw_mu, jnp.zeros((LANES - 2 * LORA,), F32)]).reshape(1, -1)
    zpad = jnp.zeros((LANES - LORA, GROUP_W), F32)
    wup_pad = jnp.concatenate([rw_w_up, zpad], axis=0)
    aup_pad = jnp.concatenate([jnp.zeros((LORA, GROUP_W), F32), rw_a_up,
                               jnp.zeros((LANES - 2 * LORA, GROUP_W), F32)], axis=0)
    o_rw = _rwkv(p, mu_pad, row(rw_w0), wup_pad, row(rw_a0), aup_pad, row(rw_k_k), row(rw_k_a),
                 row(rw_r_k), row(rw_gn_g), row(rw_gn_b), tiles["rwkv"], passes)

    kv = _memkv(mem2, row(mem_norm_g), w_mem_kv.astype(BF16), row(mem_k_g))
    two = lambda t: jnp.concatenate([t, t]).reshape(1, LANES)
    w_out_p = jnp.concatenate(
        [w_out[0:2 * GROUP_W], _perm_heads(w_out[2 * GROUP_W:3 * GROUP_W], 0), w_out[3 * GROUP_W:]],
        axis=0).astype(BF16)
    lru_prm = (conv_w, row(conv_b), _block_diag(lru_wa).astype(BF16), row(lru_ba),
               _block_diag(lru_wx).astype(BF16), row(lru_bx), row(lru_lambda))
    return _post(x2, p, o_rw, swa_sinks, lru_prm, (two(swa_q_g), two(swa_k_g)),
                 (kv, row(mem_q_g)), w_out_p, tiles["post"])


def _tiles(S):
    pick = lambda t: t if S % t == 0 else S
    return {"inproj": pick(256), "rwkv": pick(512), "post": pick(512)}


def kernel(x, mem, norm_g, w_in, conv_w, conv_b, lru_wa, lru_ba, lru_wx, lru_bx, lru_lambda, rw_mu, rw_w0, rw_w_up, rw_a0, rw_a_up, rw_k_k, rw_k_a, rw_r_k, rw_gn_g, rw_gn_b, swa_q_g, swa_k_g, swa_sinks, mem_norm_g, w_mem_kv, mem_q_g, mem_k_g, w_out):
    B, S, D = x.shape
    assert B == 1
    params = (norm_g, w_in, conv_w, conv_b, lru_wa, lru_ba, lru_wx, lru_bx, lru_lambda,
              rw_mu, rw_w0, rw_w_up, rw_a0, rw_a_up, rw_k_k, rw_k_a, rw_r_k, rw_gn_g, rw_gn_b,
              swa_q_g, swa_k_g, swa_sinks, mem_norm_g, w_mem_kv, mem_q_g, mem_k_g, w_out)
    x2 = x.reshape(S, D)
    mem2 = mem.reshape(mem.shape[1], D)
    tiles = _tiles(S)
    w_pad_all = _wprep(w_in)
    for l in range(norm_g.shape[0]):
        x2 = _layer(x2, mem2, tuple(t[l] for t in params), w_pad_all, l, tiles, passes=1)
    return x2.reshape(B, S, D)
```

```python
import functools

import jax
import jax.numpy as jnp
from jax import lax
from jax.experimental import pallas as pl
from jax.experimental.pallas import tpu as pltpu

F32 = jnp.float32
BF16 = jnp.bfloat16

LANES = 128
SUBLANES = 8
GROUP_W = 512
HEAD64 = 64
CHUNK = 64
CONV_W = 4
LRU_C = 8.0
LORA = 32
EPS = 1e-6
RWKV_GN_EPS = 64e-5
WINDOW = 128
N_MEM = 256
MEM_HEADS = 4
NEG_BIG = -1e30
VMEM_LIMIT = 56 * 1024 * 1024

COL_LRU_X, COL_LRU_G = 0, 512
COL_RW_R, COL_RW_K, COL_RW_V, COL_RW_G = 1024, 1536, 2048, 2560
COL_SWA_Q, COL_SWA_G = 3072, 3584
COL_MEM_Q, COL_MEM_G = 4096, 4608
COL_SWA_KV, COL_RW_LORA = 5120, 5376
IN_PAD = 5632
IN_HEAD = 2560

NN = (((1,), (0,)), ((), ()))
NT = (((1,), (1,)), ((), ()))
TN = (((0,), (0,)), ((), ()))
BNN = (((2,), (1,)), ((0,), (0,)))
BNT = (((2,), (2,)), ((0,), (0,)))
BTN = (((1,), (1,)), ((0,), (0,)))


def _dot(a, b, dims=NN):
    return lax.dot_general(a, b, dims, preferred_element_type=F32)


def _split(a):
    hi = a.astype(BF16)
    lo = (a - hi.astype(F32)).astype(BF16)
    return hi, lo


def _mm(a, b, dims=NN, passes=1):
    if passes == 1:
        return _dot(a.astype(BF16), b.astype(BF16), dims)
    ah, al = _split(a)
    bh, bl = _split(b)
    return _dot(ah, bh, dims) + (_dot(ah, bl, dims) + _dot(al, bh, dims))


def _mm_exact_rhs(a, b_bf16):
    ah, al = _split(a)
    return _dot(ah, b_bf16) + _dot(al, b_bf16)


def _iota2(shape, axis):
    return lax.broadcasted_iota(jnp.int32, shape, axis)


def _head_ones():
    r = _iota2((LANES, LANES), 0) // HEAD64
    c = _iota2((LANES, LANES), 1) // HEAD64
    return (r == c).astype(BF16)


def _segsum64(x, ones_bd):
    n, w = x.shape
    cols = [_mm_exact_rhs(x[:, j:j + LANES], ones_bd) for j in range(0, w, LANES)]
    return cols[0] if len(cols) == 1 else jnp.concatenate(cols, axis=1)


def _norm64(x, gain):
    ss = _segsum64(x * x, _head_ones())
    return x * lax.rsqrt(ss * (1.0 / HEAD64) + EPS) * gain


def _softplus(z):
    return jnp.maximum(z, 0.0) + jnp.log1p(jnp.exp(-jnp.abs(z)))


def _sigmoid(z):
    return 0.5 * jnp.tanh(0.5 * z) + 0.5


def _silu(z):
    return z * _sigmoid(z)


def _inproj_kernel(x_ref, g_ref, w_ref, o_ref):
    x = x_ref[...]
    ms = jnp.mean(x * x, axis=-1, keepdims=True)
    h = (x * lax.rsqrt(ms + EPS) * g_ref[...]).astype(BF16)
    o_ref[...] = _dot(h, w_ref[...])


def _inproj(x2, g, w_pad_all, layer, tm):
    S, D = x2.shape
    return pl.pallas_call(
        _inproj_kernel,
        grid=(S // tm,),
        in_specs=[
            pl.BlockSpec((tm, D), lambda i: (i, 0)),
            pl.BlockSpec((1, D), lambda i: (0, 0)),
            pl.BlockSpec((None, D, IN_PAD), lambda i: (layer, 0, 0),
                         pipeline_mode=pl.Buffered(1)),
        ],
        out_specs=pl.BlockSpec((tm, IN_PAD), lambda i: (i, 0)),
        out_shape=jax.ShapeDtypeStruct((S, IN_PAD), F32),
        compiler_params=pltpu.CompilerParams(
            dimension_semantics=("arbitrary",), vmem_limit_bytes=VMEM_LIMIT),
        name="inproj",
    )(x2, g, w_pad_all)


def _lru_block(j, x_ref, g_ref, cw_ref, cb_ref, wa_ref, ba_ref, wx_ref, bx_ref, lam_ref,
               xbuf_ref, h_ref):
    tm = x_ref.shape[0]
    c0, c1 = j * LANES, (j + 1) * LANES
    x = x_ref[:, c0:c1]
    xbuf_ref[8:8 + tm, c0:c1] = x
    conv = cb_ref[:, c0:c1] + cw_ref[CONV_W - 1:CONV_W, c0:c1] * x
    for t in range(CONV_W - 1):
        off = 8 - (CONV_W - 1) + t
        conv = conv + cw_ref[t:t + 1, c0:c1] * xbuf_ref[off:off + tm, c0:c1]
    xbuf_ref[0:8, c0:c1] = x[tm - 8:tm, :]

    cb16 = conv.astype(BF16)
    r = _sigmoid(_dot(cb16, wa_ref[c0:c1, c0:c1]) + ba_ref[:, c0:c1])
    ig = _sigmoid(_dot(cb16, wx_ref[c0:c1, c0:c1]) + bx_ref[:, c0:c1])
    log_a = (-LRU_C) * r * _softplus(-lam_ref[:, c0:c1])
    a = jnp.exp(log_a)
    m2 = jnp.maximum(-jnp.tanh(log_a) * (a * a + 1.0), 1e-12)
    u = (m2 * lax.rsqrt(m2)) * (ig * conv)

    n_slabs = tm // SUBLANES
    a3 = a.reshape(n_slabs, SUBLANES, LANES)
    u3 = u.reshape(n_slabs, SUBLANES, LANES)
    sub = lax.broadcasted_iota(jnp.int32, (1, SUBLANES, LANES), 1)
    d = 1
    while d < SUBLANES:
        keep = sub >= d
        a_sh = jnp.where(keep, pltpu.roll(a3, d, 1), 1.0)
        u_sh = jnp.where(keep, pltpu.roll(u3, d, 1), 0.0)
        u3 = u3 + a3 * u_sh
        a3 = a3 * a_sh
        d *= 2
    h = h_ref[:, c0:c1]
    slabs = []
    for s in range(n_slabs):
        h_s = u3[s] + a3[s] * h
        slabs.append(h_s)
        h = h_s[SUBLANES - 1:SUBLANES, :]
    h_ref[:, c0:c1] = h
    return jnp.concatenate(slabs, axis=0) * _silu(g_ref[:, c0:c1])


def _rwkv_rows(r0, tm, r_ref, k_ref, v_ref, l_ref, g_ref, mu_ref, w0_ref, wup_ref, a0_ref,
               aup_ref, kk_ref, ka_ref, rk_ref, gng_ref, gnb_ref, o_ref, prev_ref, state_ref, passes):
    n_chunks = tm // CHUNK
    n_pairs = GROUP_W // LANES
    WIN = 3 * GROUP_W + LANES
    row_t = _iota2((tm, 1), 0)
    last = r0 + tm == r_ref.shape[0]

    def shift_lerp(ref, c0, c1):
        x = ref[r0:r0 + tm, :]
        before = prev_ref[:, c0:c1] if r0 == 0 else ref[r0 - 1:r0, :]
        xs = jnp.where(row_t == 0, before, pltpu.roll(x, 1, 0))
        if last:
            prev_ref[:, c0:c1] = x[tm - 1:tm, :]
        return x + (xs - x) * mu_ref[:, c0:c1]

    r = shift_lerp(r_ref, 0, GROUP_W)
    k = shift_lerp(k_ref, GROUP_W, 2 * GROUP_W)
    v = shift_lerp(v_ref, 2 * GROUP_W, 3 * GROUP_W)
    lo_in = shift_lerp(l_ref, 3 * GROUP_W, WIN)

    lane_l = _iota2((tm, LANES), 1)
    lora = jnp.where(lane_l < LORA, jnp.tanh(lo_in), lo_in)
    w = -_softplus(-(w0_ref[...] + _mm(lora, wup_ref[...], passes=3))) - 0.5
    lw = -jnp.exp(w)
    a = _sigmoid(a0_ref[...] + _mm(lora, aup_ref[...], passes=3))
    yield

    ones_bd = _head_ones()
    kk = k * kk_ref[...]
    kk = kk * jnp.minimum(lax.rsqrt(_segsum64(kk * kk, ones_bd)), 1e12)
    k2 = k * (1.0 + (a - 1.0) * ka_ref[...])
    b = kk * a
    bonus = _segsum64(r * k2 * rk_ref[...], ones_bd) * v
    yield

    cl = lw
    rows_c = _iota2((tm, GROUP_W), 0) % CHUNK
    d = 1
    while d < CHUNK:
        cl = cl + jnp.where(rows_c >= d, pltpu.roll(cl, d, 0), 0.0)
        d *= 2
    yield

    to3 = lambda t: t.reshape(n_chunks, CHUNK, GROUP_W)
    cl3 = to3(cl)
    cl_end = cl3[:, CHUNK - 1:CHUNK, :]
    e_inv = jnp.exp(-cl3)
    e_end = jnp.exp(cl_end - cl3)
    w_end = jnp.exp(cl_end)
    kk3, b3, k23 = to3(kk), to3(b), to3(k2)

    lo_lane = lax.broadcasted_iota(jnp.int32, (1, CHUNK, LANES), 2) < HEAD64
    n_units = n_chunks * n_pairs
    op_dtype = BF16 if passes == 1 else F32
    cast = lambda t: t.astype(op_dtype)
    mm = functools.partial(_mm, passes=passes)

    def natural(t3):
        units = [t3[:, :, p * LANES:(p + 1) * LANES] for p in range(n_pairs)]
        return jnp.stack(units, axis=1).reshape(n_units, t3.shape[1], LANES)

    def stacked(x):
        x = cast(x)
        zero = jnp.zeros_like(x)
        return jnp.concatenate([jnp.where(lo_lane, x, zero), jnp.where(lo_lane, zero, x)], axis=1)

    a_n = natural(-kk3 * jnp.exp(cl3 - to3(lw)))
    r_n = natural(to3(r) * jnp.exp(cl3))
    a_s = stacked(a_n)
    b_s = stacked(natural(b3 * e_inv))
    k_s = stacked(natural(k23 * e_inv))
    yield
    be_s = stacked(natural(b3 * e_end))
    ke_s = stacked(natural(k23 * e_end))
    v_s = stacked(natural(to3(v)))
    wend_u = natural(w_end)
    yield

    ti = lax.broadcasted_iota(jnp.int32, (1, CHUNK, LANES), 1)
    si = lax.broadcasted_iota(jnp.int32, (1, CHUNK, LANES), 2) % CHUNK
    strict = si < ti
    incl = si <= ti
    diag16 = strict & ((ti // 16) == (si // 16))
    off32 = strict & ((ti // 32) == (si // 32)) & ((ti // 16) != (si // 16))
    off64 = strict & ((ti // 32) != (si // 32))
    eye = (si == ti).astype(F32)

    g = mm(cast(jnp.concatenate([a_n, r_n], axis=1)), jnp.concatenate([b_s, k_s], axis=1), BNT)
    g_ab, g_ak = g[:, 0:CHUNK, 0:LANES], g[:, 0:CHUNK, LANES:2 * LANES]
    g_rb, g_rk = g[:, CHUNK:2 * CHUNK, 0:LANES], g[:, CHUNK:2 * CHUNK, LANES:2 * LANES]
    l_ak = cast(jnp.where(strict, g_ak, 0.0))
    m_rb = cast(jnp.where(incl, g_rb, 0.0))
    m_rk = cast(jnp.where(incl, g_rk, 0.0))
    yield

    d1 = jnp.where(diag16, g_ab, 0.0)
    d1c = cast(d1)
    d2 = cast(mm(d1c, stacked(d1c), BNN))
    d2_s = stacked(d2)
    yield
    d4 = cast(mm(d2, d2_s, BNN))
    d4_s = stacked(d4)
    yield
    d8_s = stacked(mm(d4, d4_s, BNN))
    yield
    t = eye + d1
    t = t + mm(cast(t), d2_s, BNN)
    yield
    t = t + mm(cast(t), d4_s, BNN)
    yield
    t = t + mm(cast(t), d8_s, BNN)
    yield
    for off in (off32, off64):
        tc = cast(t)
        et = mm(cast(jnp.where(off, g_ab, 0.0)), stacked(tc), BNN)
        t = t + mm(tc, stacked(et), BNN)
        yield

    lv = mm(l_ak, v_s, BNN)
    yield
    tu = mm(cast(t), jnp.concatenate([a_s, stacked(lv)], axis=2), BNN)
    tu_s = jnp.concatenate([stacked(tu[:, :, 0:LANES]), stacked(tu[:, :, LANES:2 * LANES])], axis=2)
    yield
    ru = mm(m_rb, tu_s, BNN)
    r_hat = cast(r_n + ru[:, :, 0:LANES])
    y_v = ru[:, :, LANES:2 * LANES] + mm(m_rk, v_s, BNN)
    yield
    pq = mm(be_s, tu_s, BTN)
    pp = cast(pq[:, :, 0:LANES])
    q = pq[:, :, LANES:2 * LANES] + mm(ke_s, v_s, BTN)
    ri = lax.broadcasted_iota(jnp.int32, (1, LANES, LANES), 1)
    ci = lax.broadcasted_iota(jnp.int32, (1, LANES, LANES), 2)
    wcol = jnp.sum(jnp.where(ri == ci, wend_u, 0.0), axis=2, keepdims=True)
    yield

    s0 = state_ref[...]
    y_parts = []
    for c in range(n_chunks):
        u0, u1 = c * n_pairs, (c + 1) * n_pairs
        s0c = cast(s0)
        y_parts.append(mm(r_hat[u0:u1], s0c, BNN) + y_v[u0:u1])
        s0 = wcol[u0:u1] * s0 + mm(pp[u0:u1], s0c, BNN) + q[u0:u1]
        yield
    state_ref[...] = s0
    y4 = jnp.concatenate(y_parts, axis=0).reshape(n_chunks, n_pairs, CHUNK, LANES)
    y = jnp.concatenate([y4[:, p] for p in range(n_pairs)], axis=-1).reshape(tm, GROUP_W)
    yield
    inv_n = 1.0 / HEAD64
    mean = _segsum64(y, ones_bd) * inv_n
    yc = y - mean
    var = _segsum64(yc * yc, ones_bd) * inv_n
    yn = yc * lax.rsqrt(var + RWKV_GN_EPS) * gng_ref[...] + gnb_ref[...]
    o_ref[r0:r0 + tm, :] = (yn + bonus) * _silu(g_ref[r0:r0 + tm, :])


N_RWKV_HEAD_STAGES = 5
RWKV_CHAIN_PER_HEAD = 1
RWKV_RANGE_ROWS = 256


def _rwkv_kernel(*refs, passes, ranges):
    prev_ref, state_ref = refs[-2:]

    @pl.when(pl.program_id(0) == 0)
    def _():
        prev_ref[...] = jnp.zeros_like(prev_ref)
        state_ref[...] = jnp.zeros_like(state_ref)

    tm = refs[0].shape[0] // ranges
    streams = [_rwkv_rows(n * tm, tm, *refs, passes) for n in range(ranges)]
    def advance(g, n):
        for _ in range(n):
            if next(g, "done") == "done":
                return False
        return True

    older = []
    for g in streams:
        for _ in range(N_RWKV_HEAD_STAGES):
            older = [o for o in older if advance(o, RWKV_CHAIN_PER_HEAD)]
            advance(g, 1)
        older.append(g)
    while older:
        older = [o for o in older if advance(o, 1)]


def _rwkv(p, mu, w0, wup, a0, aup, kkw, kaw, rkw, gng, gnb, tm, passes):
    S = p.shape[0]
    row = lambda n=1, w=GROUP_W: pl.BlockSpec((n, w), lambda i: (0, 0))
    col = lambda off, w=GROUP_W: pl.BlockSpec((tm, w), lambda i: (i, off // w))
    WIN = 3 * GROUP_W + LANES
    return pl.pallas_call(
        functools.partial(_rwkv_kernel, passes=passes, ranges=max(tm // RWKV_RANGE_ROWS, 1)),
        grid=(S // tm,),
        in_specs=[
            col(COL_RW_R), col(COL_RW_K), col(COL_RW_V), col(COL_RW_LORA, LANES), col(COL_RW_G),
            row(1, WIN), row(), row(LANES), row(), row(LANES),
            row(), row(), row(), row(), row(),
        ],
        out_specs=pl.BlockSpec((tm, GROUP_W), lambda i: (i, 0)),
        out_shape=jax.ShapeDtypeStruct((S, GROUP_W), F32),
        scratch_shapes=[
            pltpu.VMEM((1, WIN), F32),
            pltpu.VMEM((GROUP_W // LANES, LANES, LANES), F32),
        ],
        compiler_params=pltpu.CompilerParams(
            dimension_semantics=("arbitrary",), vmem_limit_bytes=VMEM_LIMIT),
        name="rwkv7",
    )(p, p, p, p, p, mu, w0, wup, a0, aup, kkw, kaw, rkw, gng, gnb)


def _swa_prepare(kv_ref, kvp_ref, kg_ref):
    tq = kv_ref.shape[0]
    nblk = tq // WINDOW
    kv_all = jnp.concatenate([kvp_ref[...], kv_ref[...]], axis=0)
    k_all = _norm64(kv_all[:, 0:LANES], kg_ref[...]).astype(BF16)
    v_all = kv_all[:, LANES:2 * LANES]
    lo_v = _iota2((WINDOW + tq, LANES), 1) < HEAD64
    v_lo_all = jnp.where(lo_v, v_all, 0.0).astype(BF16)
    v_hi_all = jnp.where(lo_v, 0.0, v_all).astype(BF16)

    window = lambda t: jnp.stack(
        [t[b * WINDOW:(b + 2) * WINDOW, :] for b in range(nblk)], axis=0)

    shape3 = (nblk, 2 * WINDOW, 2 * WINDOW)
    qi = lax.broadcasted_iota(jnp.int32, shape3, 1) % WINDOW
    kj = lax.broadcasted_iota(jnp.int32, shape3, 2)
    blk = lax.broadcasted_iota(jnp.int32, shape3, 0)
    has_prev = (kj >= WINDOW) | (blk > 0) | (pl.program_id(0) > 0)
    mask = (kj > qi) & (kj <= qi + WINDOW) & has_prev
    return window(k_all), window(v_lo_all), window(v_hi_all), mask


def _swa_block(i, prepared, sink_ref, q_ref, g_ref, qg_ref):
    keys, v_lo, v_hi, mask = prepared
    tq = q_ref.shape[0]
    nblk = tq // WINDOW
    c0, c1 = i * LANES, (i + 1) * LANES
    upper_rows = lax.broadcasted_iota(jnp.int32, (1, 2 * WINDOW, 1), 1) >= WINDOW
    lo_q = lax.broadcasted_iota(jnp.int32, (1, WINDOW, LANES), 2) < HEAD64
    qn = _norm64(q_ref[:, c0:c1], qg_ref[...] * (HEAD64 ** -0.5)).reshape(nblk, WINDOW, LANES)
    q_s = jnp.concatenate([jnp.where(lo_q, qn, 0.0), jnp.where(lo_q, 0.0, qn)],
                          axis=1).astype(BF16)
    s = jnp.where(mask, _dot(q_s, keys, BNT), NEG_BIG)
    sink = jnp.where(upper_rows, sink_ref[4 + i], sink_ref[i])
    m = jnp.maximum(jnp.max(s, axis=-1, keepdims=True), sink)
    e = jnp.exp(s - m)
    rden = 1.0 / (jnp.sum(e, axis=-1, keepdims=True) + jnp.exp(sink - m))
    e = e.astype(BF16)
    out = (_dot(e[:, 0:WINDOW, :], v_lo, BNN) * rden[:, 0:WINDOW, :]
           + _dot(e[:, WINDOW:2 * WINDOW, :], v_hi, BNN) * rden[:, WINDOW:2 * WINDOW, :])
    return out.reshape(tq, LANES) * _silu(g_ref[:, c0:c1])


def _memkv_kernel(m_ref, g_ref, w_ref, kg_ref, o_ref, h_ref):
    j = pl.program_id(0)

    @pl.when(j == 0)
    def _():
        x = m_ref[...]
        ms = jnp.mean(x * x, axis=-1, keepdims=True)
        h_ref[...] = (x * lax.rsqrt(ms + EPS) * g_ref[...]).astype(BF16)

    kv = _dot(h_ref[...], w_ref[...])

    @pl.when(j < MEM_HEADS)
    def _():
        ms = jnp.mean(kv * kv, axis=-1, keepdims=True)
        o_ref[...] = kv * lax.rsqrt(ms + EPS) * kg_ref[...]

    @pl.when(j >= MEM_HEADS)
    def _():
        o_ref[...] = kv


def _memkv(mem2, g, w_kv, kg):
    M, D = mem2.shape
    N = w_kv.shape[1]
    return pl.pallas_call(
        _memkv_kernel,
        grid=(N // LANES,),
        in_specs=[
            pl.BlockSpec((M, D), lambda j: (0, 0)),
            pl.BlockSpec((1, D), lambda j: (0, 0)),
            pl.BlockSpec((D, LANES), lambda j: (0, j)),
            pl.BlockSpec((1, LANES), lambda j: (0, 0)),
        ],
        out_specs=pl.BlockSpec((M, LANES), lambda j: (0, j)),
        out_shape=jax.ShapeDtypeStruct((M, N), F32),
        scratch_shapes=[pltpu.VMEM((M, D), BF16)],
        compiler_params=pltpu.CompilerParams(
            dimension_semantics=("arbitrary",), vmem_limit_bytes=VMEM_LIMIT),
        name="memkv",
    )(mem2, g, w_kv, kg)


def _mem_head(h, q_ref, g_ref, kv_ref, qg_ref):
    head = GROUP_W // MEM_HEADS
    c0, c1 = h * head, (h + 1) * head
    q = q_ref[:, c0:c1]
    gain = qg_ref[...] * (head ** -0.5)
    qn = q * lax.rsqrt(jnp.mean(q * q, axis=-1, keepdims=True) + EPS) * gain
    s = _mm(qn, kv_ref[:, c0:c1], NT)
    m = jnp.max(s, axis=-1, keepdims=True)
    e = jnp.exp(s - m)
    rden = 1.0 / jnp.sum(e, axis=-1, keepdims=True)
    out = _mm(e, kv_ref[:, GROUP_W + c0:GROUP_W + c1]) * rden
    return out * _silu(g_ref[:, c0:c1])


def _post_kernel(sink_ref, x_ref, orw_ref, lx_ref, lg_ref, sq_ref, skv_ref, skvp_ref, sg_ref,
                 mq_ref, mg_ref, cw_ref, cb_ref, wa_ref, ba_ref, wx_ref, bx_ref, lam_ref,
                 sqg_ref, skg_ref, mkv_ref, mqg_ref, w_ref, y_ref, xbuf_ref, h_ref):
    @pl.when(pl.program_id(0) == 0)
    def _():
        xbuf_ref[0:8, :] = jnp.zeros((8, GROUP_W), F32)
        h_ref[...] = jnp.zeros_like(h_ref)

    n_stage = GROUP_W // LANES
    piece = y_ref.shape[1] // n_stage

    def project(o16, group, j):
        return _dot(o16, w_ref[group * GROUP_W:(group + 1) * GROUP_W, j * piece:(j + 1) * piece])

    def accumulate(stage_fn, o16, group, first=False):
        cols = []
        for j in range(n_stage):
            cols.append(stage_fn(j))
            n0, n1 = j * piece, (j + 1) * piece
            base = x_ref[:, n0:n1] if first else y_ref[:, n0:n1]
            y_ref[:, n0:n1] = base + project(o16, group, j)
        return jnp.concatenate(cols, axis=1).astype(BF16)

    lru = functools.partial(_lru_block, x_ref=lx_ref, g_ref=lg_ref, cw_ref=cw_ref, cb_ref=cb_ref,
                            wa_ref=wa_ref, ba_ref=ba_ref, wx_ref=wx_ref, bx_ref=bx_ref,
                            lam_ref=lam_ref, xbuf_ref=xbuf_ref, h_ref=h_ref)
    o_lru = accumulate(lru, orw_ref[...].astype(BF16), 1, first=True)
    prepared = _swa_prepare(skv_ref, skvp_ref, skg_ref)
    swa = functools.partial(_swa_block, prepared=prepared, sink_ref=sink_ref, q_ref=sq_ref,
                            g_ref=sg_ref, qg_ref=sqg_ref)
    o_swa = accumulate(swa, o_lru, 0)
    mem = functools.partial(_mem_head, q_ref=mq_ref, g_ref=mg_ref, kv_ref=mkv_ref, qg_ref=mqg_ref)
    o_mem = accumulate(mem, o_swa, 2)
    for j in range(n_stage):
        n0, n1 = j * piece, (j + 1) * piece
        y_ref[:, n0:n1] = y_ref[:, n0:n1] + project(o_mem, 3, j)


def _post(x2, p, o_rw, sinks, lru_prm, swa_prm, mem_prm, w_out, tm):
    S, D = x2.shape
    nblk = tm // WINDOW
    col = lambda off, w=GROUP_W: pl.BlockSpec((tm, w), lambda i: (i, off // w))
    const = lambda shape: pl.BlockSpec(shape, lambda i: (0,) * len(shape))
    cw, cb, wa_bd, ba, wx_bd, bx, lam = lru_prm
    qg2, kg2 = swa_prm
    kv, mqg = mem_prm
    return pl.pallas_call(
        _post_kernel,
        grid=(S // tm,),
        in_specs=[
            pl.BlockSpec(memory_space=pltpu.SMEM),
            pl.BlockSpec((tm, D), lambda i: (i, 0)),
            pl.BlockSpec((tm, GROUP_W), lambda i: (i, 0)),
            col(COL_LRU_X), col(COL_LRU_G),
            col(COL_SWA_Q), col(COL_SWA_KV, 2 * LANES),
            pl.BlockSpec((WINDOW, 2 * LANES),
                         lambda i: (jnp.maximum(i * nblk - 1, 0), COL_SWA_KV // (2 * LANES))),
            col(COL_SWA_G), col(COL_MEM_Q), col(COL_MEM_G),
            const((CONV_W, GROUP_W)), const((1, GROUP_W)),
            const((GROUP_W, GROUP_W)), const((1, GROUP_W)),
            const((GROUP_W, GROUP_W)), const((1, GROUP_W)), const((1, GROUP_W)),
            const((1, LANES)), const((1, LANES)),
            const((N_MEM, 2 * GROUP_W)), const((1, GROUP_W // MEM_HEADS)),
            pl.BlockSpec((4 * GROUP_W, D), lambda i: (0, 0), pipeline_mode=pl.Buffered(1)),
        ],
        out_specs=pl.BlockSpec((tm, D), lambda i: (i, 0)),
        out_shape=jax.ShapeDtypeStruct((S, D), F32),
        scratch_shapes=[pltpu.VMEM((tm + 8, GROUP_W), F32), pltpu.VMEM((1, GROUP_W), F32)],
        compiler_params=pltpu.CompilerParams(
            dimension_semantics=("arbitrary",), vmem_limit_bytes=VMEM_LIMIT),
        name="post",
    )(sinks, x2, o_rw, p, p, p, p, p, p, p, p, cw, cb, wa_bd, ba, wx_bd, bx, lam,
      qg2, kg2, kv, mqg, w_out)


SWA_PERM = (0, 4, 1, 5, 2, 6, 3, 7)


def _perm_heads(t, axis):
    parts = jnp.split(t, 8, axis=axis)
    return jnp.concatenate([parts[h] for h in SWA_PERM], axis=axis)


def _wprep_kernel(wt_ref, o_ref):
    tk = wt_ref.shape[1]

    def put(dst, src_rows):
        col = dst
        for start, count in src_rows:
            for r in range(0, count, LANES):
                n = min(LANES, count - r)
                o_ref[:, col:col + n] = wt_ref[start + r:start + r + n, :].T.astype(BF16)
                col += n

    put(0, [(0, IN_HEAD)])
    src_g_rw = IN_HEAD + 2 * LORA
    src_q = src_g_rw + GROUP_W
    src_kv = src_q + GROUP_W
    src_g = src_kv + 2 * LANES
    put(COL_RW_G, [(src_g_rw, GROUP_W)])
    for base, dst in ((src_q, COL_SWA_Q), (src_g, COL_SWA_G)):
        for i in range(GROUP_W // LANES):
            put(dst + i * LANES, [(base + HEAD64 * i, HEAD64), (base + HEAD64 * (4 + i), HEAD64)])
    put(COL_MEM_Q, [(src_g + GROUP_W, 2 * GROUP_W)])
    put(COL_SWA_KV, [(src_kv, 2 * LANES)])
    put(COL_RW_LORA, [(IN_HEAD, 2 * LORA)])
    pad0 = COL_RW_LORA + 2 * LORA
    o_ref[:, pad0:IN_PAD] = jnp.zeros((tk, IN_PAD - pad0), BF16)


def _wprep(w_in, tk=256):
    L, D, N = w_in.shape
    return pl.pallas_call(
        _wprep_kernel,
        grid=(L, D // tk),
        in_specs=[pl.BlockSpec((None, N, tk), lambda l, i: (l, 0, i))],
        out_specs=pl.BlockSpec((None, tk, IN_PAD), lambda l, i: (l, i, 0)),
        out_shape=jax.ShapeDtypeStruct((L, D, IN_PAD), BF16),
        compiler_params=pltpu.CompilerParams(
            dimension_semantics=("arbitrary", "arbitrary"), vmem_limit_bytes=VMEM_LIMIT),
        name="wprep",
    )(jnp.swapaxes(w_in, 1, 2))


def _block_diag(w):
    nb, bw, _ = w.shape
    eye = jnp.eye(nb, dtype=w.dtype)
    return (eye[:, None, :, None] * w[:, :, None, :]).reshape(nb * bw, nb * bw)


def _layer(x2, mem2, prm, w_pad_all, layer, tiles, passes):
    (norm_g, w_in, conv_w, conv_b, lru_wa, lru_ba, lru_wx, lru_bx, lru_lambda,
     rw_mu, rw_w0, rw_w_up, rw_a0, rw_a_up, rw_k_k, rw_k_a, rw_r_k, rw_gn_g, rw_gn_b,
     swa_q_g, swa_k_g, swa_sinks, mem_norm_g, w_mem_kv, mem_q_g, mem_k_g, w_out) = prm
    row = lambda t: t.reshape(1, -1)

    p = _inproj(x2, row(norm_g), w_pad_all, layer, tiles["inproj"])

    mu_pad = jnp.concatenate([rw_mu, jnp.zeros((LANES - 2 * LORA,), F32)]).reshape(1, -1)
    zpad = jnp.zeros((LANES - LORA, GROUP_W), F32)
    wup_pad = jnp.concatenate([rw_w_up, zpad], axis=0)
    aup_pad = jnp.concatenate([jnp.zeros((LORA, GROUP_W), F32), rw_a_up,
                               jnp.zeros((LANES - 2 * LORA, GROUP_W), F32)], axis=0)
    o_rw = _rwkv(p, mu_pad, row(rw_w0), wup_pad, row(rw_a0), aup_pad, row(rw_k_k), row(rw_k_a),
                 row(rw_r_k), row(rw_gn_g), row(rw_gn_b), tiles["rwkv"], passes)

    kv = _memkv(mem2, row(mem_norm_g), w_mem_kv.astype(BF16), row(mem_k_g))
    two = lambda t: jnp.concatenate([t, t]).reshape(1, LANES)
    w_out_p = jnp.concatenate(
        [w_out[0:2 * GROUP_W], _perm_heads(w_out[2 * GROUP_W:3 * GROUP_W], 0), w_out[3 * GROUP_W:]],
        axis=0).astype(BF16)
    lru_prm = (conv_w, row(conv_b), _block_diag(lru_wa).astype(BF16), row(lru_ba),
               _block_diag(lru_wx).astype(BF16), row(lru_bx), row(lru_lambda))
    return _post(x2, p, o_rw, swa_sinks, lru_prm, (two(swa_q_g), two(swa_k_g)),
                 (kv, row(mem_q_g)), w_out_p, tiles["post"])


def _tiles(S):
    pick = lambda t: t if S % t == 0 else S
    return {"inproj": pick(256), "rwkv": pick(1024), "post": pick(512)}


def kernel(x, mem, norm_g, w_in, conv_w, conv_b, lru_wa, lru_ba, lru_wx, lru_bx, lru_lambda, rw_mu, rw_w0, rw_w_up, rw_a0, rw_a_up, rw_k_k, rw_k_a, rw_r_k, rw_gn_g, rw_gn_b, swa_q_g, swa_k_g, swa_sinks, mem_norm_g, w_mem_kv, mem_q_g, mem_k_g, w_out):
    B, S, D = x.shape
    assert B == 1
    params = (norm_g, w_in, conv_w, conv_b, lru_wa, lru_ba, lru_wx, lru_bx, lru_lambda,
              rw_mu, rw_w0, rw_w_up, rw_a0, rw_a_up, rw_k_k, rw_k_a, rw_r_k, rw_gn_g, rw_gn_b,
              swa_q_g, swa_k_g, swa_sinks, mem_norm_g, w_mem_kv, mem_q_g, mem_k_g, w_out)
    x2 = x.reshape(S, D)
    mem2 = mem.reshape(mem.shape[1], D)
    tiles = _tiles(S)
    w_pad_all = _wprep(w_in)
    for l in range(norm_g.shape[0]):
        x2 = _layer(x2, mem2, tuple(t[l] for t in params), w_pad_all, l, tiles, passes=1)
    return x2.reshape(B, S, D)
```
